```python
import math
import jax, jax.numpy as jnp
from jax import lax
import numpy as np

D_MODEL = 1024
BATCH = 2
SEQ = 16384
DEPTH = 1
DEC_BATCH = 8
DEC_SEQ = 8192
PAST_LEN = 128

N_HEADS = 12
HEAD_DIM = 64
D_ATTN = N_HEADS * HEAD_DIM
WINDOWS = (128, 512, 2048)
DILATIONS = (1, 4, 16)
N_BUCKETS = 32
MAX_DISTANCE = 1024
D_HYENA = 768
HYENA_ORDER = 2
SHORT_WIDTH = 3
FILTER_EMB = 33
FILTER_HIDDEN = 64
FAST_DECAY_PCT = 0.3
SLOW_DECAY_PCT = 1.5
DECAY_TARGET = 0.01
EPS = 1e-6
NEG_INF = -1e30
D_PROJ = 4 * D_ATTN + (HYENA_ORDER + 2) * D_HYENA + 2 * D_MODEL
SPLIT_POINTS = (D_ATTN, 2 * D_ATTN, 3 * D_ATTN, 4 * D_ATTN,
                4 * D_ATTN + (HYENA_ORDER + 1) * D_HYENA,
                4 * D_ATTN + (HYENA_ORDER + 2) * D_HYENA,
                4 * D_ATTN + (HYENA_ORDER + 2) * D_HYENA + D_MODEL)

kernel_name = 'hybrid_dilated_attn_hyena_encoder'


def _rms_norm(x, g):
    xf = x.astype(jnp.float32)
    y = xf * lax.rsqrt(jnp.mean(xf * xf, axis=-1, keepdims=True) + EPS)
    return (y * g.astype(jnp.float32)).astype(x.dtype)


def _t5_bucket(rel):
    nb = N_BUCKETS // 2
    max_exact = nb // 2
    n = np.abs(rel)
    large = max_exact + (np.log(np.maximum(n, 1) / max_exact) / math.log(MAX_DISTANCE / max_exact)
                         * (nb - max_exact)).astype(np.int32)
    large = np.minimum(large, nb - 1)
    return ((rel > 0).astype(np.int32) * nb + np.where(n < max_exact, n, large)).astype(np.int32)


def _dilated_band_attention(q, k, v, r, half, rel_bias):
    B, L, H, Dh = q.shape
    blk = half
    Lr = L // r
    nb = -(-Lr // blk)
    Lp = nb * blk

    def to_strided(t):
        return t.reshape(B, Lr, r, H, Dh).transpose(0, 2, 1, 3, 4).reshape(B * r, Lr, H, Dh)

    def key_bands(t):
        tp = jnp.pad(to_strided(t), ((0, 0), (blk, Lp - Lr + blk), (0, 0), (0, 0)))
        tp = tp.reshape(B * r, nb + 2, blk, H, Dh)
        return jnp.concatenate([tp[:, :-2], tp[:, 1:-1], tp[:, 2:]], axis=2)

    qs = jnp.pad(to_strided(q), ((0, 0), (0, Lp - Lr), (0, 0), (0, 0))).reshape(B * r, nb, blk, H, Dh)
    kb = key_bands(k)
    vb = key_bands(v)

    a = np.arange(blk)[:, None]
    b = np.arange(3 * blk)[None, :]
    rel = b - blk - a
    kpos = np.arange(nb)[:, None] * blk - blk + np.arange(3 * blk)[None, :]
    valid = (np.abs(rel) <= half)[None] & ((kpos >= 0) & (kpos < Lr))[:, None, :]
    bias = rel_bias[_t5_bucket(rel * r)].astype(jnp.float32).transpose(2, 0, 1)

    logits = jnp.einsum('znqhd,znkhd->znhqk', qs.astype(jnp.float32), kb.astype(jnp.float32)) \
        * (1.0 / math.sqrt(Dh)) + bias[None, None]
    logits = jnp.where(valid[None, :, None], logits, NEG_INF)
    m = jnp.max(logits, axis=-1, keepdims=True)
    p = jnp.exp(logits - m)
    s = jnp.sum(p, axis=-1, keepdims=True)
    o = jnp.einsum('znhqk,znkhd->znqhd', p, vb.astype(jnp.float32)) / jnp.swapaxes(s, 2, 3)
    lse = jnp.swapaxes((m + jnp.log(s))[..., 0], 2, 3)

    o = o.reshape(B * r, Lp, H, Dh)[:, :Lr]
    lse = lse.reshape(B * r, Lp, H)[:, :Lr]
    o = o.reshape(B, r, Lr, H, Dh).swapaxes(1, 2).reshape(B, L, H, Dh)
    lse = lse.reshape(B, r, Lr, H).swapaxes(1, 2).reshape(B, L, H)
    return o, lse


def _dilated_mixture_attention(q, k, v, rel_bias):
    outs, lses = [], []
    for w, r in zip(WINDOWS, DILATIONS):
        o, lse = _dilated_band_attention(q, k, v, r, w // (2 * r), rel_bias)
        outs.append(o)
        lses.append(lse)
    alpha = jax.nn.softmax(jnp.stack(lses, axis=0), axis=0)
    return jnp.sum(alpha[..., None] * jnp.stack(outs, axis=0), axis=0)


def _short_conv(x, w, b):
    xp = jnp.pad(x, ((0, 0), (1, 1), (0, 0)))
    return xp[:, :-2] * w[0] + xp[:, 1:-1] * w[1] + xp[:, 2:] * w[2] + b


def _hyena_filters(L, w1, b1, fr1, w2, b2, fr2, w3):
    t = jnp.linspace(0.0, 1.0, L, dtype=jnp.float32)[:, None]
    bands = (FILTER_EMB - 1) // 2
    w = (2.0 * math.pi / L) * jnp.arange(L, dtype=jnp.float32)[:, None]
    f = jnp.linspace(1e-4, bands - 1, bands, dtype=jnp.float32)[None, :]
    z = jnp.concatenate([t, jnp.cos(f * w), -jnp.sin(f * w)], axis=-1)
    hdn = jnp.sin(fr1 * (z @ w1 + b1))
    hdn = jnp.sin(fr2 * (hdn @ w2 + b2))
    filt = (hdn @ w3).astype(jnp.float32).reshape(L, HYENA_ORDER, 2, D_HYENA)
    deltas = jnp.abs(jnp.linspace(math.log(DECAY_TARGET) / SLOW_DECAY_PCT,
                                  math.log(DECAY_TARGET) / FAST_DECAY_PCT, D_HYENA, dtype=jnp.float32))
    decay = jnp.exp(-t * deltas[None, :])
    return filt * decay[:, None, None, :]


def _two_sided_kernel(hf, hb):
    return jnp.concatenate([hf.at[0].add(hb[0]), jnp.zeros_like(hf[:1]), hb[:0:-1]], axis=0)


def _long_conv(u, kern, d_skip):
    L = u.shape[1]
    uf32 = u.astype(jnp.float32)
    uf = jnp.fft.rfft(uf32, n=2 * L, axis=1)
    kf = jnp.fft.rfft(kern, n=2 * L, axis=0)
    y = jnp.fft.irfft(uf * kf[None], n=2 * L, axis=1)[:, :L]
    return (y + uf32 * d_skip.astype(jnp.float32)).astype(u.dtype)


def _hyena(u, short_w, short_b, filters, hyena_d):
    u = _short_conv(u, short_w, short_b)
    parts = jnp.split(u, HYENA_ORDER + 1, axis=-1)
    z = parts[0]
    for n in range(HYENA_ORDER):
        kern = _two_sided_kernel(filters[:, n, 0], filters[:, n, 1])
        z = parts[n + 1] * _long_conv(z, kern, hyena_d[n])
    return z


def _encoder_layer(x, c, w_ada, b_ada, norm_g, w_in, short_w, short_b, filt_w1, filt_b1, filt_freq1,
                   filt_w2, filt_b2, filt_freq2, filt_w3, hyena_d, w_proj_attn, w_proj_hyena, w_out, rel_bias):
    B, L, _ = x.shape
    shift, scale, gate = jnp.split(jax.nn.silu(c) @ w_ada + b_ada, 3, axis=-1)
    h = _rms_norm(x, norm_g) * (1.0 + scale[:, None, :]) + shift[:, None, :]
    proj = h @ w_in
    q, k, v, g_attn, u_hy, g_hy, m_attn, m_hy = jnp.split(proj, SPLIT_POINTS, axis=-1)

    heads = lambda t: t.reshape(B, L, N_HEADS, HEAD_DIM)
    o_attn = _dilated_mixture_attention(heads(q), heads(k), heads(v), rel_bias)
    o_attn = o_attn.reshape(B, L, D_ATTN).astype(x.dtype)
    y_attn = (o_attn * jax.nn.silu(g_attn)) @ w_proj_attn

    filters = _hyena_filters(L, filt_w1, filt_b1, filt_freq1, filt_w2, filt_b2, filt_freq2, filt_w3)
    o_hy = _hyena(u_hy, short_w, short_b, filters, hyena_d)
    y_hy = (o_hy * jax.nn.silu(g_hy)) @ w_proj_hyena

    mixed = jax.nn.sigmoid(m_attn) * y_attn + jax.nn.sigmoid(m_hy) * y_hy
    return x + gate[:, None, :] * (mixed @ w_out)


def setup_inputs(seed: int = 0) -> dict:
    key = jax.random.key(seed)
    ks = jax.random.split(key, 24)
    nrm = lambda k, shape, s: s * jax.random.normal(k, shape, jnp.float32)
    Ld = DEPTH
    return {
        'x_prompt': nrm(ks[0], (BATCH, SEQ, D_MODEL), 1.0),
        'x_sample': nrm(ks[1], (DEC_BATCH, DEC_SEQ, D_MODEL), 1.0),
        'c_prompt': nrm(ks[2], (BATCH, D_MODEL), 1.0),
        'c_sample': nrm(ks[3], (DEC_BATCH, D_MODEL), 1.0),
        'w_ada': nrm(ks[4], (Ld, D_MODEL, 3 * D_MODEL), 0.5 * D_MODEL ** -0.5),
        'b_ada': nrm(ks[5], (Ld, 3 * D_MODEL), 0.01),
        'norm_g': 1.0 + nrm(ks[6], (Ld, D_MODEL), 0.02),
        'w_in': nrm(ks[7], (Ld, D_MODEL, D_PROJ), D_MODEL ** -0.5),
        'short_w': nrm(ks[8], (Ld, SHORT_WIDTH, (HYENA_ORDER + 1) * D_HYENA), SHORT_WIDTH ** -0.5),
        'short_b': nrm(ks[9], (Ld, (HYENA_ORDER + 1) * D_HYENA), 0.01),
        'filt_w1': nrm(ks[10], (Ld, FILTER_EMB, FILTER_HIDDEN), FILTER_EMB ** -0.5),
        'filt_b1': nrm(ks[11], (Ld, FILTER_HIDDEN), 0.1),
        'filt_freq1': 1.0 + nrm(ks[12], (Ld, FILTER_HIDDEN), 0.02),
        'filt_w2': nrm(ks[13], (Ld, FILTER_HIDDEN, FILTER_HIDDEN), FILTER_HIDDEN ** -0.5),
        'filt_b2': nrm(ks[14], (Ld, FILTER_HIDDEN), 0.1),
        'filt_freq2': 1.0 + nrm(ks[15], (Ld, FILTER_HIDDEN), 0.02),
        'filt_w3': nrm(ks[16], (Ld, FILTER_HIDDEN, HYENA_ORDER * 2 * D_HYENA), 0.02 * FILTER_HIDDEN ** -0.5),
        'hyena_d': nrm(ks[17], (Ld, HYENA_ORDER, D_HYENA), 0.1),
        'w_proj_attn': nrm(ks[18], (Ld, D_ATTN, D_MODEL), D_ATTN ** -0.5),
        'w_proj_hyena': nrm(ks[19], (Ld, D_HYENA, D_MODEL), D_HYENA ** -0.5),
        'w_out': nrm(ks[20], (Ld, D_MODEL, D_MODEL), D_MODEL ** -0.5),
        'rel_bias': nrm(ks[21], (N_BUCKETS, N_HEADS), 0.2),
        'final_g': 1.0 + nrm(ks[22], (D_MODEL,), 0.02),
    }


def reference(x_prompt, x_sample, c_prompt, c_sample, w_ada, b_ada, norm_g, w_in, short_w, short_b,
              filt_w1, filt_b1, filt_freq1, filt_w2, filt_b2, filt_freq2, filt_w3, hyena_d,
              w_proj_attn, w_proj_hyena, w_out, rel_bias, final_g):
    def trunk(x, c):
        for l in range(DEPTH):
            x = _encoder_layer(x, c, w_ada[l], b_ada[l], norm_g[l], w_in[l], short_w[l], short_b[l],
                               filt_w1[l], filt_b1[l], filt_freq1[l], filt_w2[l], filt_b2[l], filt_freq2[l],
                               filt_w3[l], hyena_d[l], w_proj_attn[l], w_proj_hyena[l], w_out[l], rel_bias)
        return _rms_norm(x, final_g)

    y_prompt = trunk(x_prompt, c_prompt)
    y_sample = trunk(x_sample, c_sample)
    return (y_prompt, y_sample)
```

```python
import functools
import math

import numpy as np
import jax
import jax.numpy as jnp
from jax import lax
from jax.experimental import pallas as pl
from jax.experimental.pallas import tpu as pltpu

F32 = jnp.float32
BF16 = jnp.bfloat16

D_MODEL = 1024
N_HEADS = 12
HEAD_DIM = 64
D_ATTN = N_HEADS * HEAD_DIM
N_PAIRS = N_HEADS // 2
DILATIONS = (1, 4, 16)
HALF = 64
N_BUCKETS = 32
MAX_DISTANCE = 1024
D_HYENA = 768
FILTER_EMB = 33
FILTER_HIDDEN = 64
FAST_DECAY_PCT = 0.3
SLOW_DECAY_PCT = 1.5
DECAY_TARGET = 0.01
EPS = 1e-6
NEG_INF = -1e30
LANES = 128
BF16_ROWS = 16
VMEM_LIMIT = 56 * 1024 * 1024

C_Q, C_K, C_V, C_GA = 0, 768, 1536, 2304
C_U = 3072
C_GH = C_U + 3 * D_HYENA
C_MA = C_GH + D_HYENA
C_MH = C_MA + D_MODEL
D_PROJ = C_MH + D_MODEL


def _params(*sem):
    return pltpu.CompilerParams(dimension_semantics=sem, vmem_limit_bytes=VMEM_LIMIT)


def _sigmoid(x):
    return 1.0 / (1.0 + jnp.exp(-x))


def _dot(a, b):
    return jnp.dot(a, b, preferred_element_type=F32)


def _ada_kernel(c_ref, w_ref, b_ref, o_ref):
    c = c_ref[...]
    s = c * _sigmoid(c)
    o_ref[...] = jnp.dot(s, w_ref[...], preferred_element_type=F32,
                         precision=lax.Precision.HIGHEST) + b_ref[...]


def _ada(c_pad, w_ada, b_ada):
    rows = c_pad.shape[0]
    return pl.pallas_call(
        _ada_kernel,
        grid=(3,),
        in_specs=[pl.BlockSpec((rows, D_MODEL), lambda j: (0, 0)),
                  pl.BlockSpec((D_MODEL, D_MODEL), lambda j: (0, j)),
                  pl.BlockSpec((1, D_MODEL), lambda j: (0, j))],
        out_specs=pl.BlockSpec((rows, D_MODEL), lambda j: (0, j)),
        out_shape=jax.ShapeDtypeStruct((rows, 3 * D_MODEL), F32),
        compiler_params=_params("arbitrary"),
        name="ada",
    )(c_pad, w_ada, b_ada.reshape(1, -1))


def _inproj_kernel(x_ref, mod_ref, g_ref, w_ref, q_ref, k_ref, v_ref, ga_ref, u_ref, gh_ref, sa_ref, sh_ref):
    x = x_ref[0]
    ms = jnp.mean(x * x, axis=-1, keepdims=True)
    y = x * lax.rsqrt(ms + EPS) * g_ref[...]
    h = (y * (1.0 + mod_ref[0, 1:2, :]) + mod_ref[0, 0:1, :]).astype(BF16)

    def seg(c0, width):
        return _dot(h, w_ref[:, c0:c0 + width])

    q_ref[0] = (seg(C_Q, D_ATTN) * (1.0 / math.sqrt(HEAD_DIM))).astype(BF16)
    k_ref[0] = seg(C_K, D_ATTN).astype(BF16)
    v_ref[0] = seg(C_V, D_ATTN).astype(BF16)
    g = seg(C_GA, D_ATTN)
    ga_ref[0] = (g * _sigmoid(g)).astype(BF16)
    for j in range(3):
        u_ref[0, :, j * D_HYENA:(j + 1) * D_HYENA] = seg(C_U + j * D_HYENA, D_HYENA).astype(BF16)
    g = seg(C_GH, D_HYENA)
    gh_ref[0] = (g * _sigmoid(g)).astype(BF16)
    sa_ref[0] = _sigmoid(seg(C_MA, D_MODEL)).astype(BF16)
    sh_ref[0] = _sigmoid(seg(C_MH, D_MODEL)).astype(BF16)


def _inproj(x, mod, norm_g, w_in_bf, tm):
    B, L, _ = x.shape
    row = lambda width: pl.BlockSpec((1, tm, width), lambda b, i: (b, i, 0))
    shp = lambda width: jax.ShapeDtypeStruct((B, L, width), BF16)
    return pl.pallas_call(
        _inproj_kernel,
        grid=(B, L // tm),
        in_specs=[row(D_MODEL),
                  pl.BlockSpec((1, 3, D_MODEL), lambda b, i: (b, 0, 0)),
                  pl.BlockSpec((1, D_MODEL), lambda b, i: (0, 0)),
                  pl.BlockSpec((D_MODEL, D_PROJ), lambda b, i: (0, 0), pipeline_mode=pl.Buffered(1))],
        out_specs=[row(D_ATTN), row(D_ATTN), row(D_ATTN), row(D_ATTN), row(3 * D_HYENA), row(D_HYENA),
                   row(D_MODEL), row(D_MODEL)],
        out_shape=[shp(D_ATTN), shp(D_ATTN), shp(D_ATTN), shp(D_ATTN), shp(3 * D_HYENA), shp(D_HYENA),
                   shp(D_MODEL), shp(D_MODEL)],
        compiler_params=_params("parallel", "parallel"),
        name="inproj",
    )(x, mod, norm_g.reshape(1, -1), w_in_bf)


def _t5_bucket(rel):
    nb = N_BUCKETS // 2
    max_exact = nb // 2
    n = np.abs(rel)
    large = max_exact + (np.log(np.maximum(n, 1) / max_exact) / math.log(MAX_DISTANCE / max_exact)
                         * (nb - max_exact)).astype(np.int32)
    large = np.minimum(large, nb - 1)
    return ((rel > 0).astype(np.int32) * nb + np.where(n < max_exact, n, large)).astype(np.int32)


def _bias_table(rel_bias, r, qb):
    a = np.arange(qb)[:, None]
    b = np.arange(qb + 2 * HALF)[None, :]
    rel = b - HALF - a
    valid = np.abs(rel) <= HALF
    bias = rel_bias[_t5_bucket(rel * r)].astype(F32)
    bias = jnp.where(valid[..., None], bias, NEG_INF).transpose(2, 0, 1)
    return bias.reshape(N_PAIRS, 2 * qb, qb + 2 * HALF)


def _attn_kernel(q_ref, kp_ref, kc_ref, kn_ref, vp_ref, vc_ref, vn_ref, bias_ref, o_ref, lse_ref, *, mb, qb, lr):
    i = pl.program_id(1)
    nsub = mb // qb
    kw = qb + 2 * HALF
    lane = lax.broadcasted_iota(jnp.int32, (qb, LANES), 1)
    lo = lane < HEAD_DIM
    kcol = lax.broadcasted_iota(jnp.int32, (1, kw), 1)

    def window(p_ref, c_ref, n_ref, s, cols):
        parts = []
        if s == 0:
            parts.append(p_ref[0, :, cols])
        lo_row = max(s * qb - HALF, 0)
        hi_row = min((s + 1) * qb + HALF, mb)
        parts.append(c_ref[0, lo_row:hi_row, cols])
        if s == nsub - 1:
            parts.append(n_ref[0, :, cols])
        return parts[0] if len(parts) == 1 else jnp.concatenate(parts, axis=0)

    for s in range(nsub):
        edge = None
        if s == 0 or s == nsub - 1:
            pos = kcol + (i * mb + (s * qb - HALF))
            edge = jnp.where(pos >= 0, jnp.where(pos < lr, 0.0, NEG_INF), NEG_INF)
        lse_tile = jnp.zeros((qb, LANES), F32)
        for p in range(N_PAIRS):
            cols = slice(p * LANES, (p + 1) * LANES)
            qp = q_ref[0, s * qb:(s + 1) * qb, cols]
            zero = jnp.zeros_like(qp)
            q2 = jnp.concatenate([jnp.where(lo, qp, zero), jnp.where(lo, zero, qp)], axis=0)
            kwin = window(kp_ref, kc_ref, kn_ref, s, cols)
            vwin = window(vp_ref, vc_ref, vn_ref, s, cols)
            logits = lax.dot_general(q2, kwin, (((1,), (1,)), ((), ())), preferred_element_type=F32)
            logits = logits + bias_ref[p]
            if edge is not None:
                logits = logits + edge
            m = jnp.max(logits, axis=-1, keepdims=True)
            e = jnp.exp(logits - m)
            l = jnp.sum(e, axis=-1, keepdims=True)
            o = _dot(e.astype(BF16), vwin) / l
            o_ref[0, s * qb:(s + 1) * qb, cols] = jnp.where(lo, o[:qb], o[qb:]).astype(BF16)
            lse = m + jnp.log(l)
            lse_tile = jnp.where(lane == 2 * p, lse[:qb], lse_tile)
            lse_tile = jnp.where(lane == 2 * p + 1, lse[qb:], lse_tile)
        lse_ref[0, s * qb:(s + 1) * qb, :] = lse_tile


def _band_attention(q, k, v, rel_bias, r):
    B, L, _ = q.shape
    lr = L // r
    qb = 128
    mb = min(512, lr)
    hb = mb // HALF
    nhalo = lr // HALF
    view = lambda t: t.reshape(B, lr, r * D_ATTN)
    cur = pl.BlockSpec((1, mb, D_ATTN), lambda b, i, rho: (b, i, rho))
    prev = pl.BlockSpec((1, HALF, D_ATTN), lambda b, i, rho: (b, jnp.maximum(i * hb - 1, 0), rho))
    nxt = pl.BlockSpec((1, HALF, D_ATTN), lambda b, i, rho: (b, jnp.minimum((i + 1) * hb, nhalo - 1), rho))
    bias = _bias_table(rel_bias, r, qb)
    o, lse = pl.pallas_call(
        functools.partial(_attn_kernel, mb=mb, qb=qb, lr=lr),
        grid=(B, lr // mb, r),
        in_specs=[cur, prev, cur, nxt, prev, cur, nxt,
                  pl.BlockSpec(bias.shape, lambda b, i, rho: (0, 0, 0))],
        out_specs=[cur, pl.BlockSpec((1, mb, LANES), lambda b, i, rho: (b, i, rho))],
        out_shape=[jax.ShapeDtypeStruct((B, lr, r * D_ATTN), BF16),
                   jax.ShapeDtypeStruct((B, lr, r * LANES), F32)],
        compiler_params=_params("parallel", "parallel", "parallel"),
        name=f"attn_r{r}",
    )(view(q), view(k), view(k), view(k), view(v), view(v), view(v), bias)
    return o.reshape(B, L, D_ATTN), lse.reshape(B, L, LANES)


def _short_kernel(up_ref, uc_ref, un_ref, w_ref, b_ref, o_ref, *, tl, nl):
    i = pl.program_id(2)
    u = uc_ref[0].astype(F32)
    row = lax.broadcasted_iota(jnp.int32, u.shape, 0)
    prev_row = jnp.where(i > 0, up_ref[0, BF16_ROWS - 1:BF16_ROWS, :].astype(F32), 0.0)
    next_row = jnp.where(i < nl - 1, un_ref[0, 0:1, :].astype(F32), 0.0)
    before = jnp.where(row == 0, prev_row, pltpu.roll(u, 1, 0))
    after = jnp.where(row == tl - 1, next_row, pltpu.roll(u, tl - 1, 0))
    out = before * w_ref[0:1, :] + u * w_ref[1:2, :] + after * w_ref[2:3, :] + b_ref[...]
    o_ref[0, 0] = out.astype(BF16)


def _short_conv(u, short_w, short_b, tl):
    B, L, _ = u.shape
    nl = L // tl
    hb = tl // BF16_ROWS
    nhalo = L // BF16_ROWS
    return pl.pallas_call(
        functools.partial(_short_kernel, tl=tl, nl=nl),
        grid=(3, B, nl),
        in_specs=[pl.BlockSpec((1, BF16_ROWS, D_HYENA), lambda j, b, i: (b, jnp.maximum(i * hb - 1, 0), j)),
                  pl.BlockSpec((1, tl, D_HYENA), lambda j, b, i: (b, i, j)),
                  pl.BlockSpec((1, BF16_ROWS, D_HYENA), lambda j, b, i: (b, jnp.minimum((i + 1) * hb, nhalo - 1), j)),
                  pl.BlockSpec((3, D_HYENA), lambda j, b, i: (0, j)),
                  pl.BlockSpec((1, D_HYENA), lambda j, b, i: (0, j))],
        out_specs=pl.BlockSpec((1, 1, tl, D_HYENA), lambda j, b, i: (j, b, i, 0)),
        out_shape=jax.ShapeDtypeStruct((3, B, L, D_HYENA), BF16),
        compiler_params=_params("parallel", "parallel", "parallel"),
        name="short_conv",
    )(u, u, u, short_w, short_b.reshape(1, -1))


def _filter_features(L):
    j = jnp.arange(2 * L, dtype=jnp.int32)
    pos = jnp.where(j < L, j, 2 * L - j).astype(F32)
    t = (pos / (L - 1))[:, None]
    bands = (FILTER_EMB - 1) // 2
    w = (2.0 * math.pi / L) * pos[:, None]
    f = jnp.linspace(1e-4, bands - 1, bands, dtype=F32)[None, :]
    z = jnp.concatenate([t, jnp.cos(f * w), -jnp.sin(f * w)], axis=-1)
    return jnp.pad(z, ((0, 0), (0, LANES - FILTER_EMB)))


def _filt_kernel(z_ref, w1_ref, b1_ref, f1_ref, w2_ref, b2_ref, f2_ref, w3a_ref, w3b_ref, w3a0_ref, w3b0_ref,
                 dl_ref, o_ref, *, tr, L):
    i = pl.program_id(0)
    h = jnp.sin(f1_ref[...] * (_dot(z_ref[...], w1_ref[...]) + b1_ref[...]))
    h = jnp.sin(f2_ref[...] * (_dot(h, w2_ref[...]) + b2_ref[...]))
    row = i * tr + lax.broadcasted_iota(jnp.int32, (tr, D_HYENA), 0)
    pos = jnp.where(row < L, row, 2 * L - row).astype(F32)
    decay = jnp.exp(-(pos / (L - 1)) * dl_ref[...])
    keep = row != L
    o_ref[0] = jnp.where(keep, _dot(h, w3a_ref[...]) * decay, 0.0)
    o_ref[1] = jnp.where(keep, _dot(h, w3b_ref[...]) * decay, 0.0)

    @pl.when(i == 0)
    def _():
        first = lax.broadcasted_iota(jnp.int32, (8, D_HYENA), 0) == 0
        o_ref[0, 0:8, :] = o_ref[0, 0:8, :] + jnp.where(first, _dot(h[0:8], w3a0_ref[...]), 0.0)
        o_ref[1, 0:8, :] = o_ref[1, 0:8, :] + jnp.where(first, _dot(h[0:8], w3b0_ref[...]), 0.0)


def _filters(L, w1, b1, f1, w2, b2, f2, w3, tr):
    pad_h = LANES - FILTER_HIDDEN
    z = _filter_features(L)
    w1p = jnp.pad(w1, ((0, LANES - FILTER_EMB), (0, pad_h)))
    w2p = jnp.pad(w2, ((0, pad_h), (0, pad_h)))
    w3p = jnp.pad(w3, ((0, pad_h), (0, 0)))
    vec = lambda a: jnp.pad(a.reshape(1, -1), ((0, 0), (0, pad_h)))
    deltas = np.abs(np.linspace(math.log(DECAY_TARGET) / SLOW_DECAY_PCT, math.log(DECAY_TARGET) / FAST_DECAY_PCT,
                                D_HYENA)).astype(np.float32).reshape(1, -1)
    nt = 2 * L // tr
    half = nt // 2
    const = lambda shape: pl.BlockSpec(shape, lambda i: (0, 0))
    w3spec = lambda n: pl.BlockSpec((LANES, D_HYENA), lambda i: (0, 2 * n + jnp.where(i < half, 0, 1)))
    w3back = lambda n: pl.BlockSpec((LANES, D_HYENA), lambda i: (0, 2 * n + 1))
    return pl.pallas_call(
        functools.partial(_filt_kernel, tr=tr, L=L),
        grid=(nt,),
        in_specs=[pl.BlockSpec((tr, LANES), lambda i: (i, 0)),
                  const((LANES, LANES)), const((1, LANES)), const((1, LANES)),
                  const((LANES, LANES)), const((1, LANES)), const((1, LANES)),
                  w3spec(0), w3spec(1), w3back(0), w3back(1), const((1, D_HYENA))],
        out_specs=pl.BlockSpec((2, tr, D_HYENA), lambda i: (0, i, 0)),
        out_shape=jax.ShapeDtypeStruct((2, 2 * L, D_HYENA), F32),
        compiler_params=_params("arbitrary"),
        name="hyena_filters",
    )(z, w1p, vec(b1), vec(f1), w2p, vec(b2), vec(f2), w3p, w3p, w3p, w3p, jnp.asarray(deltas))


def _dft_mats(n_out, n_in, n, sign):
    a = np.arange(n_out)[:, None]
    b = np.arange(n_in)[None, :]
    ang = 2.0 * np.pi * ((a * b) % n) / n
    c, s = np.cos(ang), sign * np.sin(ang)
    return np.block([[c, -s], [s, c]])


def _twiddles(n1, n2):
    ang = 2.0 * np.pi * ((np.arange(n2)[:, None] * np.arange(n1)[None, :]) % (n1 * n2)) / (n1 * n2)
    c = jnp.asarray(np.cos(ang).astype(np.float32))
    s = jnp.asarray(np.sin(ang).astype(np.float32))
    wide = lambda t: jnp.broadcast_to(t[..., None], t.shape + (LANES,))
    return (wide(c), wide(s)), (wide(c.T).reshape(n1 * n2, LANES), wide(s.T).reshape(n1 * n2, LANES))


def _s1_kernel(x_ref, g_ref, tc_ref, ts_ref, o_ref, *, n1, g, real_in):
    for j in range(g):
        cols = slice(j * D_HYENA, (j + 1) * D_HYENA)
        if real_in:
            xs = x_ref[0, :, cols].astype(BF16)
        else:
            xs = x_ref[0, 0, :, :, cols]
            xs = xs.reshape(xs.shape[0] * xs.shape[1], D_HYENA)
        a = _dot(g_ref[...], xs)
        ar, ai = a[:n1], a[n1:]
        c, s = tc_ref[j], ts_ref[j]
        for t in range(D_HYENA // LANES):
            sl = slice(t * LANES, (t + 1) * LANES)
            dst = slice(j * D_HYENA + t * LANES, j * D_HYENA + (t + 1) * LANES)
            o_ref[0, 0, :, dst] = (ar[:, sl] * c + ai[:, sl] * s).astype(BF16)
            o_ref[0, 1, :, dst] = (ai[:, sl] * c - ar[:, sl] * s).astype(BF16)


def _s1(x, sel, mat, tw, n1, n2, g, real_in):
    w = g * D_HYENA
    P = x.shape[0] if real_in else x.shape[1]
    if real_in:
        xspec = pl.BlockSpec((1, n1, w), lambda p, cb: (p, 0, cb))
    else:
        xspec = pl.BlockSpec((1, 1, 2, n1 // 2, w), lambda p, cb: (sel, p, 0, 0, cb))
    return pl.pallas_call(
        functools.partial(_s1_kernel, n1=n1, g=g, real_in=real_in),
        grid=(P, n2 // g),
        in_specs=[xspec,
                  pl.BlockSpec(mat.shape, lambda p, cb: (0, 0)),
                  pl.BlockSpec((g, n1, LANES), lambda p, cb: (cb, 0, 0)),
                  pl.BlockSpec((g, n1, LANES), lambda p, cb: (cb, 0, 0))],
        out_specs=pl.BlockSpec((1, 2, n1, w), lambda p, cb: (p, 0, 0, cb)),
        out_shape=jax.ShapeDtypeStruct((P, 2, n1, n2 * D_HYENA), BF16),
        compiler_params=_params("parallel", "parallel"),
        name="fft_level1",
    )(x, mat, tw[0], tw[1])


def _s2_filter_kernel(a_ref, gf_ref, o_ref, *, kb, n2, scale):
    for kk in range(kb):
        rows = slice(kk * n2, (kk + 1) * n2)
        ast = jnp.concatenate([a_ref[0, 0, rows, :], a_ref[0, 1, rows, :]], axis=0)
        x = _dot(gf_ref[...], ast) * scale
        o_ref[0, 0, rows, :] = x[:n2]
        o_ref[0, 1, rows, :] = x[n2:]


def _s2_filter(a, gf, n1, n2, kb):
    blk = pl.BlockSpec((1, 2, kb * n2, D_HYENA), lambda n, k: (n, 0, k, 0))
    return pl.pallas_call(
        functools.partial(_s2_filter_kernel, kb=kb, n2=n2, scale=1.0 / (n1 * n2)),
        grid=(a.shape[0], n1 // kb),
        in_specs=[blk, pl.BlockSpec(gf.shape, lambda n, k: (0, 0))],
        out_specs=blk,
        out_shape=jax.ShapeDtypeStruct(a.shape, F32),
        compiler_params=_params("parallel", "parallel"),
        name="fft_filter_level2",
    )(a, gf)


def _s2_kernel(a_ref, kf_ref, gf_ref, gi_ref, tc_ref, ts_ref, o_ref, *, kb, n2):
    for kk in range(kb):
        rows = slice(kk * n2, (kk + 1) * n2)
        ast = jnp.concatenate([a_ref[0, 0, rows, :], a_ref[0, 1, rows, :]], axis=0)
        x = _dot(gf_ref[...], ast)
        xr, xi = x[:n2], x[n2:]
        kr, ki = kf_ref[0, 0, rows, :], kf_ref[0, 1, rows, :]
        yst = jnp.concatenate([xr * kr - xi * ki, xr * ki + xi * kr], axis=0).astype(BF16)
        b = _dot(gi_ref[...], yst)
        br, bi = b[:n2], b[n2:]
        c, s = tc_ref[rows, :], ts_ref[rows, :]
        for t in range(D_HYENA // LANES):
            sl = slice(t * LANES, (t + 1) * LANES)
            o_ref[0, 0, rows, sl] = (br[:, sl] * c - bi[:, sl] * s).astype(BF16)
            o_ref[0, 1, rows, sl] = (bi[:, sl] * c + br[:, sl] * s).astype(BF16)


def _s2(a, kf, order, gf, gi, tw, n1, n2, kb):
    P = a.shape[0]
    blk = pl.BlockSpec((1, 2, kb * n2, D_HYENA), lambda k, p: (p, 0, k, 0))
    return pl.pallas_call(
        functools.partial(_s2_kernel, kb=kb, n2=n2),
        grid=(n1 // kb, P),
        in_specs=[blk,
                  pl.BlockSpec((1, 2, kb * n2, D_HYENA), lambda k, p: (order, 0, k, 0)),
                  pl.BlockSpec(gf.shape, lambda k, p: (0, 0)),
                  pl.BlockSpec(gi.shape, lambda k, p: (0, 0)),
                  pl.BlockSpec((kb * n2, LANES), lambda k, p: (k, 0)),
                  pl.BlockSpec((kb * n2, LANES), lambda k, p: (k, 0))],
        out_specs=blk,
        out_shape=jax.ShapeDtypeStruct(a.shape, BF16),
        compiler_params=_params("parallel", "parallel"),
        name="fft_level2",
    )(a, kf, gf, gi, tw[0], tw[1])


def _s3_kernel(b_ref, z_ref, x_ref, d_ref, gate_ref, g_ref, o_ref, *, n1, g):
    for j in range(g):
        cols = slice(j * D_HYENA, (j + 1) * D_HYENA)
        bst = b_ref[0, :, :, cols].reshape(2 * n1, D_HYENA)
        y = _dot(g_ref[...], bst).reshape(2, n1 // 2, D_HYENA)
        conv = y + z_ref[0, 0, :, :, cols].astype(F32) * d_ref[...]
        out = x_ref[0, 0, :, :, cols].astype(F32) * conv
        if gate_ref is not None:
            out = out * gate_ref[0, 0, :, :, cols].astype(F32)
        o_ref[0, 0, :, :, cols] = out.astype(BF16)


def _s3(b, z, zsel, x, xsel, d, gate, mat, n1, n2, g):
    P = b.shape[0]
    w = g * D_HYENA
    pair = lambda sel: pl.BlockSpec((1, 1, 2, n1 // 2, w), lambda p, cb: (sel, p, 0, 0, cb))
    in_specs = [pl.BlockSpec((1, 2, n1, w), lambda p, cb: (p, 0, 0, cb)), pair(zsel), pair(xsel),
                pl.BlockSpec((1, D_HYENA), lambda p, cb: (0, 0))]
    args = [b, z, x, d.reshape(1, -1)]
    if gate is not None:
        in_specs.append(pair(0))
        args.append(gate)
        kern = functools.partial(_s3_kernel, n1=n1, g=g)
    else:
        kern = lambda b_r, z_r, x_r, d_r, g_r, o_r: _s3_kernel(b_r, z_r, x_r, d_r, None, g_r, o_r, n1=n1, g=g)
    in_specs.append(pl.BlockSpec(mat.shape, lambda p, cb: (0, 0)))
    args.append(mat)
    return pl.pallas_call(
        kern,
        grid=(P, n2 // g),
        in_specs=in_specs,
        out_specs=pair(0),
        out_shape=jax.ShapeDtypeStruct((1, P, 2, n1 // 2, n2 * D_HYENA), BF16),
        compiler_params=_params("parallel", "parallel"),
        name="fft_level3",
    )(*args)


def _hyena(u, gate_h, short_w, short_b, filt, hyena_d, cfg):
    B, L, _ = u.shape
    n1, n2 = cfg["n1"], cfg["n2"]
    P = B // 2
    bf = lambda m: jnp.asarray(m, dtype=BF16)
    g1f = bf(_dft_mats(n1, n1 // 2, n1, -1))
    g1k = bf(_dft_mats(n1, n1, n1, -1)[:, :n1])
    g2f = bf(_dft_mats(n2, n2, n2, -1))
    g2i = bf(_dft_mats(n2, n2, n2, +1))
    g3i = bf(_dft_mats(n1 // 2, n1, n1, +1))
    tw1, tw2 = _twiddles(n1, n2)

    kern = _filters(L, *filt, tr=cfg["tr"])
    ka = _s1(kern.reshape(2, n1, n2 * D_HYENA), 0, g1k, tw1, n1, n2, cfg["g1"], True)
    kf = _s2_filter(ka.reshape(2, 2, n1 * n2, D_HYENA), g2f, n1, n2, cfg["kb"])

    zs = _short_conv(u, short_w, short_b, cfg["tl"])
    pair_view = lambda t: t.reshape(t.shape[0], P, 2, n1 // 2, n2 * D_HYENA)
    zs = pair_view(zs)
    gate_h = pair_view(gate_h[None])
    z, zsel = zs, 0
    for n in range(2):
        a = _s1(z, zsel, g1f, tw1, n1, n2, cfg["g1"], False)
        b = _s2(a.reshape(P, 2, n1 * n2, D_HYENA), kf, n, g2f, g2i, tw2, n1, n2, cfg["kb"])
        z = _s3(b.reshape(P, 2, n1, n2 * D_HYENA), z, zsel, zs, n + 1, hyena_d[n],
                gate_h if n == 1 else None, g3i, n1, n2, cfg["g3"])
        zsel = 0
    return z.reshape(B, L, D_HYENA)


def _out_kernel(o1_ref, o4_ref, o16_ref, l1_ref, l4_ref, l16_ref, ga_ref, hg_ref, sa_ref, sh_ref, x_ref, mod_ref,
                fg_ref, e_ref, wpa_ref, wph_ref, wo_ref, out_ref):
    l1, l4, l16 = l1_ref[0], l4_ref[0], l16_ref[0]
    mx = jnp.maximum(jnp.maximum(l1, l4), l16)
    e1, e4, e16 = jnp.exp(l1 - mx), jnp.exp(l4 - mx), jnp.exp(l16 - mx)
    tot = e1 + e4 + e16
    heads = lambda a: _dot((a / tot).astype(BF16), e_ref[...])
    oa = (heads(e1) * o1_ref[0].astype(F32) + heads(e4) * o4_ref[0].astype(F32)
          + heads(e16) * o16_ref[0].astype(F32))
    ya = _dot((oa * ga_ref[0].astype(F32)).astype(BF16), wpa_ref[...])
    yh = _dot(hg_ref[0], wph_ref[...])
    mixed = sa_ref[0].astype(F32) * ya + sh_ref[0].astype(F32) * yh
    y = x_ref[0] + mod_ref[0, 2:3, :] * _dot(mixed.astype(BF16), wo_ref[...])
    ms = jnp.mean(y * y, axis=-1, keepdims=True)
    out_ref[0] = y * lax.rsqrt(ms + EPS) * fg_ref[...]


def _out_proj(o_pat, lse_pat, ga, hg, sa, sh, x, mod, final_g, wpa, wph, wo, tm):
    B, L, _ = x.shape
    row = lambda width: pl.BlockSpec((1, tm, width), lambda b, i: (b, i, 0))
    const = lambda shape: pl.BlockSpec(shape, lambda b, i: (0,) * len(shape))
    expand = np.zeros((LANES, D_ATTN), np.float32)
    expand[np.arange(D_ATTN) // HEAD_DIM, np.arange(D_ATTN)] = 1.0
    return pl.pallas_call(
        _out_kernel,
        grid=(B, L // tm),
        in_specs=[row(D_ATTN)] * 3 + [row(LANES)] * 3 + [row(D_ATTN), row(D_HYENA), row(D_MODEL), row(D_MODEL),
                                                        row(D_MODEL),
                                                        pl.BlockSpec((1, 3, D_MODEL), lambda b, i: (b, 0, 0)),
                                                        const((1, D_MODEL)), const((LANES, D_ATTN)),
                                                        const((D_ATTN, D_MODEL)), const((D_HYENA, D_MODEL)),
                                                        const((D_MODEL, D_MODEL))],
        out_specs=row(D_MODEL),
        out_shape=jax.ShapeDtypeStruct((B, L, D_MODEL), F32),
        compiler_params=_params("parallel", "parallel"),
        name="out_proj",
    )(*o_pat, *lse_pat, ga, hg, sa, sh, x, mod, final_g.reshape(1, -1), jnp.asarray(expand, dtype=BF16),
      wpa, wph, wo)


def _group_config(L):
    n = 2 * L
    n1 = 256 if n >= 32768 else 128 if n >= 16384 else 32
    n2 = n // n1
    return dict(n1=n1, n2=n2, g1=min(4, n2), g3=min(4, n2), kb=min(4, n1), tr=min(512, 2 * L), tl=min(512, L),
                tm=min(512, L))


def _trunk(x, mod, wts):
    B, L, _ = x.shape
    cfg = _group_config(L)
    q, k, v, ga, u, gh, sa, sh = _inproj(x, mod, wts["norm_g"], wts["w_in"], cfg["tm"])
    pats = [_band_attention(q, k, v, wts["rel_bias"], r) for r in DILATIONS]
    hg = _hyena(u, gh, wts["short_w"], wts["short_b"], wts["filt"], wts["hyena_d"], cfg)
    return _out_proj([p[0] for p in pats], [p[1] for p in pats], ga, hg, sa, sh, x, mod, wts["final_g"],
                     wts["w_proj_attn"], wts["w_proj_hyena"], wts["w_out"], cfg["tm"])


def kernel(x_prompt, x_sample, c_prompt, c_sample, w_ada, b_ada, norm_g, w_in, short_w, short_b, filt_w1, filt_b1,
           filt_freq1, filt_w2, filt_b2, filt_freq2, filt_w3, hyena_d, w_proj_attn, w_proj_hyena, w_out, rel_bias,
           final_g):
    assert w_ada.shape[0] == 1, "single layer"
    bp, bs = c_prompt.shape[0], c_sample.shape[0]
    c_all = jnp.concatenate([c_prompt, c_sample], axis=0)
    rows = -(-(bp + bs) // 8) * 8
    c_all = jnp.pad(c_all, ((0, rows - bp - bs), (0, 0)))
    mod = _ada(c_all, w_ada[0], b_ada[0]).reshape(rows, 3, D_MODEL)
    wts = dict(norm_g=norm_g[0], w_in=w_in[0].astype(BF16), short_w=short_w[0], short_b=short_b[0],
               filt=(filt_w1[0], filt_b1[0], filt_freq1[0], filt_w2[0], filt_b2[0], filt_freq2[0], filt_w3[0]),
               hyena_d=hyena_d[0], w_proj_attn=w_proj_attn[0].astype(BF16),
               w_proj_hyena=w_proj_hyena[0].astype(BF16), w_out=w_out[0].astype(BF16), rel_bias=rel_bias,
               final_g=final_g)
    y_prompt = _trunk(x_prompt, mod[:bp], wts)
    y_sample = _trunk(x_sample, mod[bp:bp + bs], wts)
    return (y_prompt, y_sample)
```

```python
import functools
import math

import numpy as np
import jax
import jax.numpy as jnp
from jax import lax
from jax.experimental import pallas as pl
from jax.experimental.pallas import tpu as pltpu

F32 = jnp.float32
BF16 = jnp.bfloat16

D_MODEL = 1024
N_HEADS = 12
HEAD_DIM = 64
D_ATTN = N_HEADS * HEAD_DIM
N_PAIRS = N_HEADS // 2
DILATIONS = (1, 4, 16)
HALF = 64
N_BUCKETS = 32
MAX_DISTANCE = 1024
D_HYENA = 768
FILTER_EMB = 33
FILTER_HIDDEN = 64
FAST_DECAY_PCT = 0.3
SLOW_DECAY_PCT = 1.5
DECAY_TARGET = 0.01
EPS = 1e-6
NEG_INF = -1e30
LANES = 128
SUBLANES = 8
BF16_ROWS = 16
VMEM_LIMIT = 56 * 1024 * 1024
N_CGROUPS = D_HYENA // SUBLANES

C_Q, C_K, C_V, C_GA = 0, 768, 1536, 2304
C_U = 3072
C_GH = C_U + 3 * D_HYENA
C_MA = C_GH + D_HYENA
C_MH = C_MA + D_MODEL
D_PROJ = C_MH + D_MODEL
R_Q, R_K, R_V, R_GA, R_MA, R_MH = 0, 768, 1536, 2304, 3072, 4096
D_ROWMAJOR = 5120

NT_DIMS = (((1,), (1,)), ((), ()))
TN_DIMS = (((0,), (0,)), ((), ()))


def _params(*sem):
    return pltpu.CompilerParams(dimension_semantics=sem, vmem_limit_bytes=VMEM_LIMIT)


def _sigmoid(x):
    return 1.0 / (1.0 + jnp.exp(-x))


def _dot(a, b):
    return jnp.dot(a, b, preferred_element_type=F32)


def _to_channel_tiled(ref, lead, row0, value):
    groups = value.shape[0] // SUBLANES
    for j in range(value.shape[1] // LANES):
        rows = slice(row0 + j * SUBLANES, row0 + (j + 1) * SUBLANES)
        ref[lead + (rows, slice(None))] = (
            value[:, j * LANES:(j + 1) * LANES].reshape(groups, SUBLANES, LANES))


def _ada_kernel(c_ref, w_ref, b_ref, o_ref):
    c = c_ref[...]
    s = c * _sigmoid(c)
    o_ref[...] = jnp.dot(s, w_ref[...], preferred_element_type=F32,
                         precision=lax.Precision.HIGHEST) + b_ref[...]


def _ada(c_pad, w_ada, b_ada):
    rows = c_pad.shape[0]
    return pl.pallas_call(
        _ada_kernel,
        grid=(3,),
        in_specs=[pl.BlockSpec((rows, D_MODEL), lambda j: (0, 0)),
                  pl.BlockSpec((D_MODEL, D_MODEL), lambda j: (0, j)),
                  pl.BlockSpec((1, D_MODEL), lambda j: (0, j))],
        out_specs=pl.BlockSpec((rows, D_MODEL), lambda j: (0, j)),
        out_shape=jax.ShapeDtypeStruct((rows, 3 * D_MODEL), F32),
        compiler_params=_params("arbitrary"),
        name="ada",
    )(c_pad, w_ada, b_ada.reshape(1, -1))


def _inproj_kernel(x_ref, mod_ref, g_ref, w_ref, wt_ref, q_ref, k_ref, v_ref, ga_ref, sa_ref, sh_ref, u_ref, gh_ref):
    x = x_ref[0]
    ms = jnp.mean(x * x, axis=-1, keepdims=True)
    y = x * lax.rsqrt(ms + EPS) * g_ref[...]
    h = (y * (1.0 + mod_ref[0, 1:2, :]) + mod_ref[0, 0:1, :]).astype(BF16)

    def seg(c0, width):
        return _dot(h, w_ref[:, c0:c0 + width])

    q_ref[0] = (seg(R_Q, D_ATTN) * (1.0 / math.sqrt(HEAD_DIM))).astype(BF16)
    k_ref[0] = seg(R_K, D_ATTN).astype(BF16)
    v_ref[0] = seg(R_V, D_ATTN).astype(BF16)
    g = seg(R_GA, D_ATTN)
    ga_ref[0] = (g * _sigmoid(g)).astype(BF16)
    sa_ref[0] = _sigmoid(seg(R_MA, D_MODEL)).astype(BF16)
    sh_ref[0] = _sigmoid(seg(R_MH, D_MODEL)).astype(BF16)
    for j in range(3):
        ut = lax.dot_general(wt_ref[j * D_HYENA:(j + 1) * D_HYENA, :], h, NT_DIMS, preferred_element_type=F32)
        _to_channel_tiled(u_ref, (0, slice(j * N_CGROUPS, (j + 1) * N_CGROUPS)), 0, ut)
    g = lax.dot_general(wt_ref[3 * D_HYENA:4 * D_HYENA, :], h, NT_DIMS, preferred_element_type=F32)
    _to_channel_tiled(gh_ref, (0, slice(None)), 0, g * _sigmoid(g))


def _inproj(x, mod, norm_g, w_rows, w_hy_t, tm):
    B, L, _ = x.shape
    rows_ct = tm // LANES * SUBLANES
    row = lambda width: pl.BlockSpec((1, tm, width), lambda b, i: (b, i, 0))
    shp = lambda width: jax.ShapeDtypeStruct((B, L, width), BF16)
    ct = lambda groups: pl.BlockSpec((1, groups, rows_ct, LANES), lambda b, i: (b, 0, i, 0))
    ct_shape = lambda groups: jax.ShapeDtypeStruct((B, groups, L // LANES * SUBLANES, LANES), F32)
    resident = lambda shape: pl.BlockSpec(shape, lambda b, i: (0, 0), pipeline_mode=pl.Buffered(1))
    return pl.pallas_call(
        _inproj_kernel,
        grid=(B, L // tm),
        in_specs=[row(D_MODEL),
                  pl.BlockSpec((1, 3, D_MODEL), lambda b, i: (b, 0, 0)),
                  pl.BlockSpec((1, D_MODEL), lambda b, i: (0, 0)),
                  resident((D_MODEL, D_ROWMAJOR)), resident((4 * D_HYENA, D_MODEL))],
        out_specs=[row(D_ATTN), row(D_ATTN), row(D_ATTN), row(D_ATTN), row(D_MODEL), row(D_MODEL),
                   ct(3 * N_CGROUPS), ct(N_CGROUPS)],
        out_shape=[shp(D_ATTN), shp(D_ATTN), shp(D_ATTN), shp(D_ATTN), shp(D_MODEL), shp(D_MODEL),
                   ct_shape(3 * N_CGROUPS), ct_shape(N_CGROUPS)],
        compiler_params=_params("parallel", "parallel"),
        name="inproj",
    )(x, mod, norm_g.reshape(1, -1), w_rows, w_hy_t)


def _t5_bucket(rel):
    nb = N_BUCKETS // 2
    max_exact = nb // 2
    n = np.abs(rel)
    large = max_exact + (np.log(np.maximum(n, 1) / max_exact) / math.log(MAX_DISTANCE / max_exact)
                         * (nb - max_exact)).astype(np.int32)
    large = np.minimum(large, nb - 1)
    return ((rel > 0).astype(np.int32) * nb + np.where(n < max_exact, n, large)).astype(np.int32)


def _bias_table(rel_bias, r, qb):
    a = np.arange(qb)[:, None]
    b = np.arange(qb + 2 * HALF)[None, :]
    rel = b - HALF - a
    valid = np.abs(rel) <= HALF
    bias = rel_bias[_t5_bucket(rel * r)].astype(F32)
    bias = jnp.where(valid[..., None], bias, NEG_INF).transpose(2, 0, 1)
    return bias.reshape(N_PAIRS, 2 * qb, qb + 2 * HALF)


def _attn_kernel(q_ref, kp_ref, kc_ref, kn_ref, vp_ref, vc_ref, vn_ref, bias_ref, o_ref, lse_ref, *, mb, qb, lr):
    i = pl.program_id(1)
    nsub = mb // qb
    kw = qb + 2 * HALF
    lane = lax.broadcasted_iota(jnp.int32, (qb, LANES), 1)
    lo = lane < HEAD_DIM
    kcol = lax.broadcasted_iota(jnp.int32, (1, kw), 1)

    def window(p_ref, c_ref, n_ref, s, cols):
        parts = []
        if s == 0:
            parts.append(p_ref[0, :, cols])
        lo_row = max(s * qb - HALF, 0)
        hi_row = min((s + 1) * qb + HALF, mb)
        parts.append(c_ref[0, lo_row:hi_row, cols])
        if s == nsub - 1:
            parts.append(n_ref[0, :, cols])
        return parts[0] if len(parts) == 1 else jnp.concatenate(parts, axis=0)

    for s in range(nsub):
        edge = None
        if s == 0 or s == nsub - 1:
            pos = kcol + (i * mb + (s * qb - HALF))
            edge = jnp.where(pos >= 0, jnp.where(pos < lr, 0.0, NEG_INF), NEG_INF)
        lse_tile = jnp.zeros((qb, LANES), F32)
        for p in range(N_PAIRS):
            cols = slice(p * LANES, (p + 1) * LANES)
            qp = q_ref[0, s * qb:(s + 1) * qb, cols]
            zero = jnp.zeros_like(qp)
            q2 = jnp.concatenate([jnp.where(lo, qp, zero), jnp.where(lo, zero, qp)], axis=0)
            kwin = window(kp_ref, kc_ref, kn_ref, s, cols)
            vwin = window(vp_ref, vc_ref, vn_ref, s, cols)
            logits = lax.dot_general(q2, kwin, NT_DIMS, preferred_element_type=F32)
            logits = logits + bias_ref[p]
            if edge is not None:
                logits = logits + edge
            m = jnp.max(logits, axis=-1, keepdims=True)
            e = jnp.exp(logits - m)
            l = jnp.sum(e, axis=-1, keepdims=True)
            o = _dot(e.astype(BF16), vwin) / l
            o_ref[0, s * qb:(s + 1) * qb, cols] = jnp.where(lo, o[:qb], o[qb:]).astype(BF16)
            lse = m + jnp.log(l)
            lse_tile = jnp.where(lane == 2 * p, lse[:qb], lse_tile)
            lse_tile = jnp.where(lane == 2 * p + 1, lse[qb:], lse_tile)
        lse_ref[0, s * qb:(s + 1) * qb, :] = lse_tile


def _band_attention(q, k, v, rel_bias, r):
    B, L, _ = q.shape
    lr = L // r
    qb = 128
    mb = min(512, lr)
    hb = mb // HALF
    nhalo = lr // HALF
    view = lambda t: t.reshape(B, lr, r * D_ATTN)
    cur = pl.BlockSpec((1, mb, D_ATTN), lambda b, i, rho: (b, i, rho))
    prev = pl.BlockSpec((1, HALF, D_ATTN), lambda b, i, rho: (b, jnp.maximum(i * hb - 1, 0), rho))
    nxt = pl.BlockSpec((1, HALF, D_ATTN), lambda b, i, rho: (b, jnp.minimum((i + 1) * hb, nhalo - 1), rho))
    bias = _bias_table(rel_bias, r, qb)
    o, lse = pl.pallas_call(
        functools.partial(_attn_kernel, mb=mb, qb=qb, lr=lr),
        grid=(B, lr // mb, r),
        in_specs=[cur, prev, cur, nxt, prev, cur, nxt,
                  pl.BlockSpec(bias.shape, lambda b, i, rho: (0, 0, 0))],
        out_specs=[cur, pl.BlockSpec((1, mb, LANES), lambda b, i, rho: (b, i, rho))],
        out_shape=[jax.ShapeDtypeStruct((B, lr, r * D_ATTN), BF16),
                   jax.ShapeDtypeStruct((B, lr, r * LANES), F32)],
        compiler_params=_params("parallel", "parallel", "parallel"),
        name=f"attn_r{r}",
    )(view(q), view(k), view(k), view(k), view(v), view(v), view(v), bias)
    return o.reshape(B, L, D_ATTN), lse.reshape(B, L, LANES)


def _filter_features(L):
    j = jnp.arange(2 * L, dtype=jnp.int32)
    pos = jnp.where(j < L, j, 2 * L - j).astype(F32)
    t = (pos / (L - 1))[:, None]
    bands = (FILTER_EMB - 1) // 2
    w = (2.0 * math.pi / L) * pos[:, None]
    f = jnp.linspace(1e-4, bands - 1, bands, dtype=F32)[None, :]
    z = jnp.concatenate([t, jnp.cos(f * w), -jnp.sin(f * w)], axis=-1)
    return jnp.pad(z, ((0, 0), (0, LANES - FILTER_EMB)))


def _filt_kernel(z_ref, w1_ref, b1_ref, f1_ref, w2_ref, b2_ref, f2_ref, w3a_ref, w3b_ref, w3a0_ref, w3b0_ref,
                 dl_ref, o_ref, *, tr, L):
    i = pl.program_id(0)
    h = jnp.sin(f1_ref[...] * (_dot(z_ref[...], w1_ref[...]) + b1_ref[...]))
    h = jnp.sin(f2_ref[...] * (_dot(h, w2_ref[...]) + b2_ref[...]))
    lane = lax.broadcasted_iota(jnp.int32, (1, LANES), 1)
    for n, (w_ref, w0_ref) in enumerate(((w3a_ref, w3a0_ref), (w3b_ref, w3b0_ref))):
        kt = lax.dot_general(w_ref[...], h, NT_DIMS, preferred_element_type=F32)
        for j in range(tr // LANES):
            col = i * tr + j * LANES + lane
            pos = jnp.where(col < L, col, 2 * L - col).astype(F32)
            decay = jnp.exp(-(pos / (L - 1)) * dl_ref[...])
            slab = jnp.where(col != L, kt[:, j * LANES:(j + 1) * LANES] * decay, 0.0)
            o_ref[n, :, j * SUBLANES:(j + 1) * SUBLANES, :] = slab.reshape(N_CGROUPS, SUBLANES, LANES)

        @pl.when(i == 0)
        def _():
            back = lax.dot_general(w0_ref[...], h[0:LANES], NT_DIMS, preferred_element_type=F32)
            add = jnp.where(lane == 0, back, 0.0).reshape(N_CGROUPS, SUBLANES, LANES)
            o_ref[n, :, 0:SUBLANES, :] = o_ref[n, :, 0:SUBLANES, :] + add


def _filters(L, w1, b1, f1, w2, b2, f2, w3, tr):
    pad_h = LANES - FILTER_HIDDEN
    z = _filter_features(L)
    w1p = jnp.pad(w1, ((0, LANES - FILTER_EMB), (0, pad_h)))
    w2p = jnp.pad(w2, ((0, pad_h), (0, pad_h)))
    w3t = jnp.pad(w3, ((0, pad_h), (0, 0))).T
    vec = lambda a: jnp.pad(a.reshape(1, -1), ((0, 0), (0, pad_h)))
    deltas = np.abs(np.linspace(math.log(DECAY_TARGET) / SLOW_DECAY_PCT, math.log(DECAY_TARGET) / FAST_DECAY_PCT,
                                D_HYENA)).astype(np.float32)
    deltas = jnp.asarray(np.broadcast_to(deltas[:, None], (D_HYENA, LANES)))
    nt = 2 * L // tr
    half = nt // 2
    const = lambda shape: pl.BlockSpec(shape, lambda i: (0, 0))
    w3spec = lambda n: pl.BlockSpec((D_HYENA, LANES), lambda i: (2 * n + jnp.where(i < half, 0, 1), 0))
    w3back = lambda n: pl.BlockSpec((D_HYENA, LANES), lambda i: (2 * n + 1, 0))
    rows_ct = tr // LANES * SUBLANES
    return pl.pallas_call(
        functools.partial(_filt_kernel, tr=tr, L=L),
        grid=(nt,),
        in_specs=[pl.BlockSpec((tr, LANES), lambda i: (i, 0)),
                  const((LANES, LANES)), const((1, LANES)), const((1, LANES)),
                  const((LANES, LANES)), const((1, LANES)), const((1, LANES)),
                  w3spec(0), w3spec(1), w3back(0), w3back(1), const((D_HYENA, LANES))],
        out_specs=pl.BlockSpec((2, N_CGROUPS, rows_ct, LANES), lambda i: (0, 0, i, 0)),
        out_shape=jax.ShapeDtypeStruct((2, N_CGROUPS, 2 * L // LANES * SUBLANES, LANES), F32),
        compiler_params=_params("arbitrary"),
        name="hyena_filters",
    )(z, w1p, vec(b1), vec(f1), w2p, vec(b2), vec(f2), w3t, w3t, w3t, w3t, deltas)


def _dft_left(n_out, n_in, n, sign):
    a = np.arange(n_out)[:, None]
    b = np.arange(n_in)[None, :]
    ang = 2.0 * np.pi * ((a * b) % n) / n
    c, s = np.cos(ang), sign * np.sin(ang)
    return np.block([[c, -s], [s, c]])


def _dft_right(n, sign):
    return _dft_left(n, n, n, sign).T


def _hyena_kernel(v_ref, x1_ref, x2_ref, gate_ref, kt_ref, sw_ref, sb_ref, d_ref, f1_ref, f1k_ref, f2f_ref, f2i_ref,
                  f3_ref, tc_ref, ts_ref, o_ref, kr_ref, ki_ref, *, n1):
    n1h = n1 // 2
    n2 = LANES
    nch = SUBLANES
    tc, ts = tc_ref[...], ts_ref[...]

    def channel_rows(ref2d, c, n):
        return ref2d[pl.ds(c, n, stride=SUBLANES), :]

    def lanes(mats):
        return jnp.concatenate(mats, axis=1)

    def level2(ar, ai, mat_ref):
        x = _dot(lanes([ar, ai]).astype(BF16), mat_ref[...])
        return x[:, :n2], x[:, n2:]

    @pl.when(pl.program_id(1) == 0)
    def _():
        for n in range(2):
            kmat = lanes([channel_rows(kt_ref.at[n, 0], c, n1) for c in range(nch)]).astype(BF16)
            a = _dot(f1k_ref[...], kmat)
            for c in range(nch):
                ar, ai = a[:n1, c * n2:(c + 1) * n2], a[n1:, c * n2:(c + 1) * n2]
                xr, xi = level2(ar * tc + ai * ts, ai * tc - ar * ts, f2f_ref)
                kr_ref[n, c] = xr * (1.0 / (n1 * n2))
                ki_ref[n, c] = xi * (1.0 / (n1 * n2))

    lane = lax.broadcasted_iota(jnp.int32, (n1h, n2), 1)
    row = lax.broadcasted_iota(jnp.int32, (n1h, n2), 0)

    def short_conv(ref, part, b, c):
        m = channel_rows(ref.at[b, 0], c, n1h)
        left = pltpu.roll(m, 1, 1)
        before = jnp.where(lane == 0, jnp.where(row == 0, 0.0, pltpu.roll(left, 1, 0)), left)
        right = pltpu.roll(m, n2 - 1, 1)
        after = jnp.where(lane == n2 - 1, jnp.where(row == n1h - 1, 0.0, pltpu.roll(right, n1h - 1, 0)), right)
        tap = lambda k: sw_ref[k, part, c:c + 1, :]
        return before * tap(0) + m * tap(1) + after * tap(2) + sb_ref[part, c:c + 1, :]

    z = [[short_conv(v_ref, 0, b, c) for c in range(nch)] for b in range(2)]
    for n in range(2):
        stacked = jnp.concatenate([lanes(z[0]), lanes(z[1])], axis=0).astype(BF16)
        a = _dot(f1_ref[...], stacked)
        brs, bis = [], []
        for c in range(nch):
            ar, ai = a[:n1, c * n2:(c + 1) * n2], a[n1:, c * n2:(c + 1) * n2]
            xr, xi = level2(ar * tc + ai * ts, ai * tc - ar * ts, f2f_ref)
            kr, ki = kr_ref[n, c], ki_ref[n, c]
            br, bi = level2(xr * kr - xi * ki, xr * ki + xi * kr, f2i_ref)
            brs.append(br * tc - bi * ts)
            bis.append(bi * tc + br * ts)
        y = _dot(f3_ref[...], jnp.concatenate([lanes(brs), lanes(bis)], axis=0).astype(BF16))
        x_ref = x1_ref if n == 0 else x2_ref
        for b in range(2):
            for c in range(nch):
                conv = y[b * n1h:(b + 1) * n1h, c * n2:(c + 1) * n2] + z[b][c] * d_ref[n, c:c + 1, :]
                nxt = short_conv(x_ref, n + 1, b, c) * conv
                if n == 0:
                    z[b][c] = nxt
                else:
                    gate = channel_rows(gate_ref.at[b, 0], c, n1h)
                    o_ref.at[b, 0][pl.ds(c, n1h, stride=SUBLANES), :] = nxt * gate


def _hyena(u_ct, gate_ct, kt_ct, short_w, short_b, hyena_d, n1):
    B, _, R, _ = u_ct.shape
    n1h, n2 = n1 // 2, LANES
    bf = lambda m: jnp.asarray(m, dtype=BF16)
    f1 = bf(_dft_left(n1, n1h, n1, -1))
    f1k = bf(_dft_left(n1, n1, n1, -1)[:, :n1])
    f2f, f2i = bf(_dft_right(n2, -1)), bf(_dft_right(n2, +1))
    f3 = bf(_dft_left(n1h, n1, n1, +1))
    ang = 2.0 * np.pi * ((np.arange(n1)[:, None] * np.arange(n2)[None, :]) % (n1 * n2)) / (n1 * n2)
    tc, ts = jnp.asarray(np.cos(ang), dtype=F32), jnp.asarray(np.sin(ang), dtype=F32)
    wide = lambda t: jnp.broadcast_to(t[..., None], t.shape + (LANES,))
    sw = wide(short_w).reshape(3, 3, N_CGROUPS, SUBLANES, LANES)
    sb = wide(short_b).reshape(3, N_CGROUPS, SUBLANES, LANES)
    dd = wide(hyena_d).reshape(2, N_CGROUPS, SUBLANES, LANES)

    part = lambda j: pl.BlockSpec((2, 1, R, LANES), lambda g, p: (p, g + j * N_CGROUPS, 0, 0))
    const = lambda a: pl.BlockSpec(a.shape, lambda g, p: (0,) * a.ndim)
    mats = (f1, f1k, f2f, f2i, f3, tc, ts)
    return pl.pallas_call(
        functools.partial(_hyena_kernel, n1=n1),
        grid=(N_CGROUPS, B // 2),
        in_specs=[part(0), part(1), part(2),
                  pl.BlockSpec((2, 1, R, LANES), lambda g, p: (p, g, 0, 0)),
                  pl.BlockSpec((2, 1, 2 * R, LANES), lambda g, p: (0, g, 0, 0)),
                  pl.BlockSpec((3, 3, None, SUBLANES, LANES), lambda g, p: (0, 0, g, 0, 0)),
                  pl.BlockSpec((3, None, SUBLANES, LANES), lambda g, p: (0, g, 0, 0)),
                  pl.BlockSpec((2, None, SUBLANES, LANES), lambda g, p: (0, g, 0, 0))]
                 + [const(m) for m in mats],
        out_specs=pl.BlockSpec((2, 1, R, LANES), lambda g, p: (p, g, 0, 0)),
        out_shape=jax.ShapeDtypeStruct((B, N_CGROUPS, R, LANES), F32),
        scratch_shapes=[pltpu.VMEM((2, SUBLANES, n1, n2), F32), pltpu.VMEM((2, SUBLANES, n1, n2), F32)],
        compiler_params=_params("parallel", "arbitrary"),
        name="hyena",
    )(u_ct, u_ct, u_ct, gate_ct, kt_ct, sw, sb, dd, *mats)


def _out_kernel(o1_ref, o4_ref, o16_ref, l1_ref, l4_ref, l16_ref, ga_ref, hg_ref, sa_ref, sh_ref, x_ref, mod_ref,
                fg_ref, e_ref, wpa_ref, wph_ref, wo_ref, out_ref):
    l1, l4, l16 = l1_ref[0], l4_ref[0], l16_ref[0]
    mx = jnp.maximum(jnp.maximum(l1, l4), l16)
    e1, e4, e16 = jnp.exp(l1 - mx), jnp.exp(l4 - mx), jnp.exp(l16 - mx)
    tot = e1 + e4 + e16
    heads = lambda a: _dot((a / tot).astype(BF16), e_ref[...])
    oa = (heads(e1) * o1_ref[0].astype(F32) + heads(e4) * o4_ref[0].astype(F32)
          + heads(e16) * o16_ref[0].astype(F32))
    ya = _dot((oa * ga_ref[0].astype(F32)).astype(BF16), wpa_ref[...])
    tm = x_ref.shape[1]
    yh = jnp.concatenate(
        [lax.dot_general(hg_ref[0, :, j * SUBLANES:(j + 1) * SUBLANES, :].reshape(D_HYENA, LANES).astype(BF16),
                         wph_ref[...], TN_DIMS, preferred_element_type=F32) for j in range(tm // LANES)], axis=0)
    mixed = sa_ref[0].astype(F32) * ya + sh_ref[0].astype(F32) * yh
    y = x_ref[0] + mod_ref[0, 2:3, :] * _dot(mixed.astype(BF16), wo_ref[...])
    ms = jnp.mean(y * y, axis=-1, keepdims=True)
    out_ref[0] = y * lax.rsqrt(ms + EPS) * fg_ref[...]


def _out_proj(o_pat, lse_pat, ga, hg_ct, sa, sh, x, mod, final_g, wpa, wph, wo, tm):
    B, L, _ = x.shape
    row = lambda width: pl.BlockSpec((1, tm, width), lambda b, i: (b, i, 0))
    const = lambda shape: pl.BlockSpec(shape, lambda b, i: (0,) * len(shape))
    expand = np.zeros((LANES, D_ATTN), np.float32)
    expand[np.arange(D_ATTN) // HEAD_DIM, np.arange(D_ATTN)] = 1.0
    return pl.pallas_call(
        _out_kernel,
        grid=(B, L // tm),
        in_specs=[row(D_ATTN)] * 3 + [row(LANES)] * 3
                 + [row(D_ATTN),
                    pl.BlockSpec((1, N_CGROUPS, tm // LANES * SUBLANES, LANES), lambda b, i: (b, 0, i, 0)),
                    row(D_MODEL), row(D_MODEL), row(D_MODEL),
                    pl.BlockSpec((1, 3, D_MODEL), lambda b, i: (b, 0, 0)),
                    const((1, D_MODEL)), const((LANES, D_ATTN)),
                    const((D_ATTN, D_MODEL)), const((D_HYENA, D_MODEL)), const((D_MODEL, D_MODEL))],
        out_specs=row(D_MODEL),
        out_shape=jax.ShapeDtypeStruct((B, L, D_MODEL), F32),
        compiler_params=_params("parallel", "parallel"),
        name="out_proj",
    )(*o_pat, *lse_pat, ga, hg_ct, sa, sh, x, mod, final_g.reshape(1, -1), jnp.asarray(expand, dtype=BF16),
      wpa, wph, wo)


def _trunk(x, mod, wts):
    B, L, _ = x.shape
    assert L % 512 == 0 and B % 2 == 0
    tm = 512
    q, k, v, ga, sa, sh, u_ct, gh_ct = _inproj(x, mod, wts["norm_g"], wts["w_rows"], wts["w_hy_t"], tm)
    pats = [_band_attention(q, k, v, wts["rel_bias"], r) for r in DILATIONS]
    kt_ct = _filters(L, *wts["filt"], tr=512)
    hg_ct = _hyena(u_ct, gh_ct, kt_ct, wts["short_w"], wts["short_b"], wts["hyena_d"], 2 * L // LANES)
    return _out_proj([p[0] for p in pats], [p[1] for p in pats], ga, hg_ct, sa, sh, x, mod, wts["final_g"],
                     wts["w_proj_attn"], wts["w_proj_hyena"], wts["w_out"], tm)


def kernel(x_prompt, x_sample, c_prompt, c_sample, w_ada, b_ada, norm_g, w_in, short_w, short_b, filt_w1, filt_b1,
           filt_freq1, filt_w2, filt_b2, filt_freq2, filt_w3, hyena_d, w_proj_attn, w_proj_hyena, w_out, rel_bias,
           final_g):
    assert w_ada.shape[0] == 1, "single layer"
    bp, bs = c_prompt.shape[0], c_sample.shape[0]
    c_all = jnp.concatenate([c_prompt, c_sample], axis=0)
    rows = -(-(bp + bs) // SUBLANES) * SUBLANES
    c_all = jnp.pad(c_all, ((0, rows - bp - bs), (0, 0)))
    mod = _ada(c_all, w_ada[0], b_ada[0]).reshape(rows, 3, D_MODEL)
    w = w_in[0].astype(BF16)
    wts = dict(norm_g=norm_g[0],
               w_rows=jnp.concatenate([w[:, :C_U], w[:, C_MA:]], axis=1),
               w_hy_t=w[:, C_U:C_MA].T,
               short_w=short_w[0], short_b=short_b[0],
               filt=(filt_w1[0], filt_b1[0], filt_freq1[0], filt_w2[0], filt_b2[0], filt_freq2[0], filt_w3[0]),
               hyena_d=hyena_d[0], w_proj_attn=w_proj_attn[0].astype(BF16),
               w_proj_hyena=w_proj_hyena[0].astype(BF16), w_out=w_out[0].astype(BF16), rel_bias=rel_bias,
               final_g=final_g)
    y_prompt = _trunk(x_prompt, mod[:bp], wts)
    y_sample = _trunk(x_sample, mod[bp:bp + bs], wts)
    return (y_prompt, y_sample)
```

```python
import functools
import math

import numpy as np
import jax
import jax.numpy as jnp
from jax import lax
from jax.experimental import pallas as pl
from jax.experimental.pallas import tpu as pltpu

F32 = jnp.float32
BF16 = jnp.bfloat16

D_MODEL = 1024
N_HEADS = 12
HEAD_DIM = 64
D_ATTN = N_HEADS * HEAD_DIM
N_PAIRS = N_HEADS // 2
DILATIONS = (1, 4, 16)
HALF = 64
N_BUCKETS = 32
MAX_DISTANCE = 1024
D_HYENA = 768
FILTER_EMB = 33
FILTER_HIDDEN = 64
FAST_DECAY_PCT = 0.3
SLOW_DECAY_PCT = 1.5
DECAY_TARGET = 0.01
EPS = 1e-6
NEG_INF = -1e30
LANES = 128
SUBLANES = 8
BF16_ROWS = 16
VMEM_LIMIT = 56 * 1024 * 1024
N_CGROUPS = D_HYENA // SUBLANES

C_Q, C_K, C_V, C_GA = 0, 768, 1536, 2304
C_U = 3072
C_GH = C_U + 3 * D_HYENA
C_MA = C_GH + D_HYENA
C_MH = C_MA + D_MODEL
D_PROJ = C_MH + D_MODEL
R_Q, R_K, R_V, R_GA, R_MA, R_MH = 0, 768, 1536, 2304, 3072, 4096
D_ROWMAJOR = 5120

NT_DIMS = (((1,), (1,)), ((), ()))
TN_DIMS = (((0,), (0,)), ((), ()))


def _params(*sem):
    return pltpu.CompilerParams(dimension_semantics=sem, vmem_limit_bytes=VMEM_LIMIT)


def _sigmoid(x):
    return 1.0 / (1.0 + jnp.exp(-x))


def _dot(a, b):
    return jnp.dot(a, b, preferred_element_type=F32)


def _to_channel_tiled(ref, lead, row0, value):
    groups = value.shape[0] // SUBLANES
    for j in range(value.shape[1] // LANES):
        rows = slice(row0 + j * SUBLANES, row0 + (j + 1) * SUBLANES)
        ref[lead + (rows, slice(None))] = (
            value[:, j * LANES:(j + 1) * LANES].reshape(groups, SUBLANES, LANES))


def _ada_kernel(c_ref, w_ref, b_ref, o_ref):
    c = c_ref[...]
    s = c * _sigmoid(c)
    o_ref[...] = jnp.dot(s, w_ref[...], preferred_element_type=F32,
                         precision=lax.Precision.HIGHEST) + b_ref[...]


def _ada(c_pad, w_ada, b_ada):
    rows = c_pad.shape[0]
    return pl.pallas_call(
        _ada_kernel,
        grid=(3,),
        in_specs=[pl.BlockSpec((rows, D_MODEL), lambda j: (0, 0)),
                  pl.BlockSpec((D_MODEL, D_MODEL), lambda j: (0, j)),
                  pl.BlockSpec((1, D_MODEL), lambda j: (0, j))],
        out_specs=pl.BlockSpec((rows, D_MODEL), lambda j: (0, j)),
        out_shape=jax.ShapeDtypeStruct((rows, 3 * D_MODEL), F32),
        compiler_params=_params("arbitrary"),
        name="ada",
    )(c_pad, w_ada, b_ada.reshape(1, -1))


def _inproj_kernel(x_ref, mod_ref, g_ref, w_ref, wt_ref, q1_ref, k1_ref, v1_ref, q4_ref, k4_ref, v4_ref, q16_ref,
                   k16_ref, v16_ref, ga_ref, sa_ref, sh_ref, u_ref, gh_ref, slab_ref):
    x = x_ref[0]
    tm = x.shape[0]
    ms = jnp.mean(x * x, axis=-1, keepdims=True)
    y = x * lax.rsqrt(ms + EPS) * g_ref[...]
    h = (y * (1.0 + mod_ref[0, 1:2, :]) + mod_ref[0, 0:1, :]).astype(BF16)

    def seg(c0, width):
        return _dot(h, w_ref[:, c0:c0 + width])

    def store_strided(val, natural_ref, strided_refs):
        natural_ref[0, 0] = val.astype(BF16)
        for t in range(D_ATTN // LANES):
            slab_ref[t] = val[:, t * LANES:(t + 1) * LANES]
        for r, o_ref in zip(DILATIONS[1:], strided_refs):
            for rho in range(r):
                rows = [slab_ref.at[t][pl.ds(rho, tm // r, stride=r), :] for t in range(D_ATTN // LANES)]
                o_ref[0, rho] = jnp.concatenate(rows, axis=1).astype(BF16)

    store_strided(seg(R_Q, D_ATTN) * (1.0 / math.sqrt(HEAD_DIM)), q1_ref, (q4_ref, q16_ref))
    store_strided(seg(R_K, D_ATTN), k1_ref, (k4_ref, k16_ref))
    store_strided(seg(R_V, D_ATTN), v1_ref, (v4_ref, v16_ref))
    g = seg(R_GA, D_ATTN)
    ga_ref[0] = (g * _sigmoid(g)).astype(BF16)
    sa_ref[0] = _sigmoid(seg(R_MA, D_MODEL)).astype(BF16)
    sh_ref[0] = _sigmoid(seg(R_MH, D_MODEL)).astype(BF16)
    for j in range(3):
        ut = lax.dot_general(wt_ref[j * D_HYENA:(j + 1) * D_HYENA, :], h, NT_DIMS, preferred_element_type=F32)
        _to_channel_tiled(u_ref, (0, slice(j * N_CGROUPS, (j + 1) * N_CGROUPS)), 0, ut)
    g = lax.dot_general(wt_ref[3 * D_HYENA:4 * D_HYENA, :], h, NT_DIMS, preferred_element_type=F32)
    _to_channel_tiled(gh_ref, (0, slice(None)), 0, g * _sigmoid(g))


def _inproj(x, mod, norm_g, w_rows, w_hy_t, tm):
    B, L, _ = x.shape
    rows_ct = tm // LANES * SUBLANES
    row = lambda width: pl.BlockSpec((1, tm, width), lambda b, i: (b, i, 0))
    shp = lambda width: jax.ShapeDtypeStruct((B, L, width), BF16)
    strided = lambda r: pl.BlockSpec((1, r, tm // r, D_ATTN), lambda b, i: (b, 0, i, 0))
    strided_shape = lambda r: jax.ShapeDtypeStruct((B, r, L // r, D_ATTN), BF16)
    ct = lambda groups: pl.BlockSpec((1, groups, rows_ct, LANES), lambda b, i: (b, 0, i, 0))
    ct_shape = lambda groups: jax.ShapeDtypeStruct((B, groups, L // LANES * SUBLANES, LANES), F32)
    resident = lambda shape: pl.BlockSpec(shape, lambda b, i: (0, 0), pipeline_mode=pl.Buffered(1))
    qkv = [r for r in DILATIONS for _ in range(3)]
    return pl.pallas_call(
        _inproj_kernel,
        grid=(B, L // tm),
        in_specs=[row(D_MODEL),
                  pl.BlockSpec((1, 3, D_MODEL), lambda b, i: (b, 0, 0)),
                  pl.BlockSpec((1, D_MODEL), lambda b, i: (0, 0)),
                  resident((D_MODEL, D_ROWMAJOR)), resident((4 * D_HYENA, D_MODEL))],
        out_specs=[strided(r) for r in qkv]
                  + [row(D_ATTN), row(D_MODEL), row(D_MODEL), ct(3 * N_CGROUPS), ct(N_CGROUPS)],
        out_shape=[strided_shape(r) for r in qkv]
                  + [shp(D_ATTN), shp(D_MODEL), shp(D_MODEL), ct_shape(3 * N_CGROUPS), ct_shape(N_CGROUPS)],
        scratch_shapes=[pltpu.VMEM((D_ATTN // LANES, tm, LANES), F32)],
        compiler_params=_params("parallel", "parallel"),
        name="inproj",
    )(x, mod, norm_g.reshape(1, -1), w_rows, w_hy_t)


def _t5_bucket(rel):
    nb = N_BUCKETS // 2
    max_exact = nb // 2
    n = np.abs(rel)
    large = max_exact + (np.log(np.maximum(n, 1) / max_exact) / math.log(MAX_DISTANCE / max_exact)
                         * (nb - max_exact)).astype(np.int32)
    large = np.minimum(large, nb - 1)
    return ((rel > 0).astype(np.int32) * nb + np.where(n < max_exact, n, large)).astype(np.int32)


def _bias_table(rel_bias, r, qb):
    a = np.arange(qb)[:, None]
    b = np.arange(qb + 2 * HALF)[None, :]
    rel = b - HALF - a
    valid = np.abs(rel) <= HALF
    onehot = (jnp.asarray(_t5_bucket(rel * r))[..., None] == jnp.arange(N_BUCKETS)).astype(F32)
    bias = jnp.dot(onehot, rel_bias.astype(F32), precision=lax.Precision.HIGHEST)
    bias = jnp.where(valid[..., None], bias, NEG_INF).transpose(2, 0, 1)
    return bias.reshape(N_PAIRS, 2 * qb, qb + 2 * HALF)


def _attn_kernel(q_ref, kp_ref, kc_ref, kn_ref, vp_ref, vc_ref, vn_ref, bias_ref, o_ref, lse_ref, *, mb, qb, lr):
    i = pl.program_id(1)
    nsub = mb // qb
    kw = qb + 2 * HALF
    lane = lax.broadcasted_iota(jnp.int32, (qb, LANES), 1)
    lo = lane < HEAD_DIM
    kcol = lax.broadcasted_iota(jnp.int32, (1, kw), 1)

    def window(p_ref, c_ref, n_ref, s, cols):
        parts = []
        if s == 0:
            parts.append(p_ref[:, cols])
        lo_row = max(s * qb - HALF, 0)
        hi_row = min((s + 1) * qb + HALF, mb)
        parts.append(c_ref[lo_row:hi_row, cols])
        if s == nsub - 1:
            parts.append(n_ref[:, cols])
        return parts[0] if len(parts) == 1 else jnp.concatenate(parts, axis=0)

    for s in range(nsub):
        edge = None
        if s == 0 or s == nsub - 1:
            pos = kcol + (i * mb + (s * qb - HALF))
            edge = jnp.where(pos >= 0, jnp.where(pos < lr, 0.0, NEG_INF), NEG_INF)
        lse_tile = jnp.zeros((qb, LANES), F32)
        for p in range(N_PAIRS):
            cols = slice(p * LANES, (p + 1) * LANES)
            qp = q_ref[s * qb:(s + 1) * qb, cols]
            zero = jnp.zeros_like(qp)
            q2 = jnp.concatenate([jnp.where(lo, qp, zero), jnp.where(lo, zero, qp)], axis=0)
            kwin = window(kp_ref, kc_ref, kn_ref, s, cols)
            vwin = window(vp_ref, vc_ref, vn_ref, s, cols)
            logits = lax.dot_general(q2, kwin, NT_DIMS, preferred_element_type=F32)
            logits = logits + bias_ref[p]
            if edge is not None:
                logits = logits + edge
            m = jnp.max(logits, axis=-1, keepdims=True)
            e = jnp.exp(logits - m)
            l = jnp.sum(e, axis=-1, keepdims=True)
            o = _dot(e.astype(BF16), vwin) / l
            o_ref[s * qb:(s + 1) * qb, cols] = jnp.where(lo, o[:qb], o[qb:]).astype(BF16)
            lse = m + jnp.log(l)
            lse_tile = jnp.where(lane == 2 * p, lse[:qb], lse_tile)
            lse_tile = jnp.where(lane == 2 * p + 1, lse[qb:], lse_tile)
        lse_ref[s * qb:(s + 1) * qb, :] = lse_tile


def _band_attention(q, k, v, rel_bias):
    B, r, lr, _ = q.shape
    qb = 128
    mb = min(512, lr)
    hb = mb // HALF
    nhalo = lr // HALF
    blk = lambda rows, width, idx: pl.BlockSpec((None, None, rows, width), lambda b, i, rho: (b, rho, idx(i), 0))
    cur = blk(mb, D_ATTN, lambda i: i)
    prev = blk(HALF, D_ATTN, lambda i: jnp.maximum(i * hb - 1, 0))
    nxt = blk(HALF, D_ATTN, lambda i: jnp.minimum((i + 1) * hb, nhalo - 1))
    bias = _bias_table(rel_bias, r, qb)
    return pl.pallas_call(
        functools.partial(_attn_kernel, mb=mb, qb=qb, lr=lr),
        grid=(B, lr // mb, r),
        in_specs=[cur, prev, cur, nxt, prev, cur, nxt,
                  pl.BlockSpec(bias.shape, lambda b, i, rho: (0, 0, 0))],
        out_specs=[cur, blk(mb, LANES, lambda i: i)],
        out_shape=[jax.ShapeDtypeStruct((B, r, lr, D_ATTN), BF16),
                   jax.ShapeDtypeStruct((B, r, lr, LANES), F32)],
        compiler_params=_params("parallel", "parallel", "parallel"),
        name=f"attn_r{r}",
    )(q, k, k, k, v, v, v, bias)


def _filter_features(L):
    j = jnp.arange(2 * L, dtype=jnp.int32)
    pos = jnp.where(j < L, j, 2 * L - j).astype(F32)
    t = (pos / (L - 1))[:, None]
    bands = (FILTER_EMB - 1) // 2
    w = (2.0 * math.pi / L) * pos[:, None]
    f = jnp.linspace(1e-4, bands - 1, bands, dtype=F32)[None, :]
    z = jnp.concatenate([t, jnp.cos(f * w), -jnp.sin(f * w)], axis=-1)
    return jnp.pad(z, ((0, 0), (0, LANES - FILTER_EMB)))


def _filt_kernel(z_ref, w1_ref, b1_ref, f1_ref, w2_ref, b2_ref, f2_ref, w3a_ref, w3b_ref, w3a0_ref, w3b0_ref,
                 dl_ref, o_ref, *, tr, L):
    i = pl.program_id(0)
    h = jnp.sin(f1_ref[...] * (_dot(z_ref[...], w1_ref[...]) + b1_ref[...]))
    h = jnp.sin(f2_ref[...] * (_dot(h, w2_ref[...]) + b2_ref[...]))
    lane = lax.broadcasted_iota(jnp.int32, (1, LANES), 1)
    for n, (w_ref, w0_ref) in enumerate(((w3a_ref, w3a0_ref), (w3b_ref, w3b0_ref))):
        kt = lax.dot_general(w_ref[...], h, NT_DIMS, preferred_element_type=F32)
        for j in range(tr // LANES):
            col = i * tr + j * LANES + lane
            pos = jnp.where(col < L, col, 2 * L - col).astype(F32)
            decay = jnp.exp(-(pos / (L - 1)) * dl_ref[...])
            slab = jnp.where(col != L, kt[:, j * LANES:(j + 1) * LANES] * decay, 0.0)
            o_ref[n, :, j * SUBLANES:(j + 1) * SUBLANES, :] = slab.reshape(N_CGROUPS, SUBLANES, LANES)

        @pl.when(i == 0)
        def _():
            back = lax.dot_general(w0_ref[...], h[0:LANES], NT_DIMS, preferred_element_type=F32)
            add = jnp.where(lane == 0, back, 0.0).reshape(N_CGROUPS, SUBLANES, LANES)
            o_ref[n, :, 0:SUBLANES, :] = o_ref[n, :, 0:SUBLANES, :] + add


def _filters(L, w1, b1, f1, w2, b2, f2, w3, tr):
    pad_h = LANES - FILTER_HIDDEN
    z = _filter_features(L)
    w1p = jnp.pad(w1, ((0, LANES - FILTER_EMB), (0, pad_h)))
    w2p = jnp.pad(w2, ((0, pad_h), (0, pad_h)))
    w3t = jnp.pad(w3, ((0, pad_h), (0, 0))).T
    vec = lambda a: jnp.pad(a.reshape(1, -1), ((0, 0), (0, pad_h)))
    deltas = np.abs(np.linspace(math.log(DECAY_TARGET) / SLOW_DECAY_PCT, math.log(DECAY_TARGET) / FAST_DECAY_PCT,
                                D_HYENA)).astype(np.float32)
    deltas = jnp.asarray(np.broadcast_to(deltas[:, None], (D_HYENA, LANES)))
    nt = 2 * L // tr
    half = nt // 2
    const = lambda shape: pl.BlockSpec(shape, lambda i: (0, 0))
    w3spec = lambda n: pl.BlockSpec((D_HYENA, LANES), lambda i: (2 * n + jnp.where(i < half, 0, 1), 0))
    w3back = lambda n: pl.BlockSpec((D_HYENA, LANES), lambda i: (2 * n + 1, 0))
    rows_ct = tr // LANES * SUBLANES
    return pl.pallas_call(
        functools.partial(_filt_kernel, tr=tr, L=L),
        grid=(nt,),
        in_specs=[pl.BlockSpec((tr, LANES), lambda i: (i, 0)),
                  const((LANES, LANES)), const((1, LANES)), const((1, LANES)),
                  const((LANES, LANES)), const((1, LANES)), const((1, LANES)),
                  w3spec(0), w3spec(1), w3back(0), w3back(1), const((D_HYENA, LANES))],
        out_specs=pl.BlockSpec((2, N_CGROUPS, rows_ct, LANES), lambda i: (0, 0, i, 0)),
        out_shape=jax.ShapeDtypeStruct((2, N_CGROUPS, 2 * L // LANES * SUBLANES, LANES), F32),
        compiler_params=_params("arbitrary"),
        name="hyena_filters",
    )(z, w1p, vec(b1), vec(f1), w2p, vec(b2), vec(f2), w3t, w3t, w3t, w3t, deltas)


def _dft_left(n_out, n_in, n, sign):
    a = np.arange(n_out)[:, None]
    b = np.arange(n_in)[None, :]
    ang = 2.0 * np.pi * ((a * b) % n) / n
    c, s = np.cos(ang), sign * np.sin(ang)
    return np.block([[c, -s], [s, c]])


def _dft_right(n, sign):
    return _dft_left(n, n, n, sign).T


def _hyena_kernel(v_ref, x1_ref, x2_ref, gate_ref, kt_ref, sw_ref, sb_ref, d_ref, f1_ref, f1k_ref, f2f_ref, f2i_ref,
                  f3_ref, tc_ref, ts_ref, o_ref, kr_ref, ki_ref, *, n1):
    n1h = n1 // 2
    n2 = LANES
    nch = SUBLANES
    tc, ts = tc_ref[...], ts_ref[...]

    def channel_rows(ref2d, c, n):
        return ref2d[pl.ds(c, n, stride=SUBLANES), :]

    def lanes(mats):
        return jnp.concatenate(mats, axis=1)

    def level2(ar, ai, mat_ref):
        x = _dot(lanes([ar, ai]).astype(BF16), mat_ref[...])
        return x[:, :n2], x[:, n2:]

    @pl.when(pl.program_id(1) == 0)
    def _():
        for n in range(2):
            kmat = lanes([channel_rows(kt_ref.at[n, 0], c, n1) for c in range(nch)]).astype(BF16)
            a = _dot(f1k_ref[...], kmat)
            for c in range(nch):
                ar, ai = a[:n1, c * n2:(c + 1) * n2], a[n1:, c * n2:(c + 1) * n2]
                xr, xi = level2(ar * tc + ai * ts, ai * tc - ar * ts, f2f_ref)
                kr_ref[n, c] = xr * (1.0 / (n1 * n2))
                ki_ref[n, c] = xi * (1.0 / (n1 * n2))

    lane = lax.broadcasted_iota(jnp.int32, (n1h, n2), 1)
    row = lax.broadcasted_iota(jnp.int32, (n1h, n2), 0)

    def short_conv(ref, part, b, c):
        m = channel_rows(ref.at[b, 0], c, n1h)
        left = pltpu.roll(m, 1, 1)
        before = jnp.where(lane == 0, jnp.where(row == 0, 0.0, pltpu.roll(left, 1, 0)), left)
        right = pltpu.roll(m, n2 - 1, 1)
        after = jnp.where(lane == n2 - 1, jnp.where(row == n1h - 1, 0.0, pltpu.roll(right, n1h - 1, 0)), right)
        tap = lambda k: sw_ref[k, part, c:c + 1, :]
        return before * tap(0) + m * tap(1) + after * tap(2) + sb_ref[part, c:c + 1, :]

    z = [[short_conv(v_ref, 0, b, c) for c in range(nch)] for b in range(2)]
    for n in range(2):
        stacked = jnp.concatenate([lanes(z[0]), lanes(z[1])], axis=0).astype(BF16)
        a = _dot(f1_ref[...], stacked)
        brs, bis = [], []
        for c in range(nch):
            ar, ai = a[:n1, c * n2:(c + 1) * n2], a[n1:, c * n2:(c + 1) * n2]
            xr, xi = level2(ar * tc + ai * ts, ai * tc - ar * ts, f2f_ref)
            kr, ki = kr_ref[n, c], ki_ref[n, c]
            br, bi = level2(xr * kr - xi * ki, xr * ki + xi * kr, f2i_ref)
            brs.append(br * tc - bi * ts)
            bis.append(bi * tc + br * ts)
        y = _dot(f3_ref[...], jnp.concatenate([lanes(brs), lanes(bis)], axis=0).astype(BF16))
        x_ref = x1_ref if n == 0 else x2_ref
        for b in range(2):
            for c in range(nch):
                conv = y[b * n1h:(b + 1) * n1h, c * n2:(c + 1) * n2] + z[b][c] * d_ref[n, c:c + 1, :]
                nxt = short_conv(x_ref, n + 1, b, c) * conv
                if n == 0:
                    z[b][c] = nxt
                else:
                    gate = channel_rows(gate_ref.at[b, 0], c, n1h)
                    o_ref.at[b, 0][pl.ds(c, n1h, stride=SUBLANES), :] = nxt * gate


def _hyena(u_ct, gate_ct, kt_ct, short_w, short_b, hyena_d, n1):
    B, _, R, _ = u_ct.shape
    n1h, n2 = n1 // 2, LANES
    bf = lambda m: jnp.asarray(m, dtype=BF16)
    f1 = bf(_dft_left(n1, n1h, n1, -1))
    f1k = bf(_dft_left(n1, n1, n1, -1)[:, :n1])
    f2f, f2i = bf(_dft_right(n2, -1)), bf(_dft_right(n2, +1))
    f3 = bf(_dft_left(n1h, n1, n1, +1))
    ang = 2.0 * np.pi * ((np.arange(n1)[:, None] * np.arange(n2)[None, :]) % (n1 * n2)) / (n1 * n2)
    tc, ts = jnp.asarray(np.cos(ang), dtype=F32), jnp.asarray(np.sin(ang), dtype=F32)
    wide = lambda t: jnp.broadcast_to(t[..., None], t.shape + (LANES,))
    sw = wide(short_w).reshape(3, 3, N_CGROUPS, SUBLANES, LANES)
    sb = wide(short_b).reshape(3, N_CGROUPS, SUBLANES, LANES)
    dd = wide(hyena_d).reshape(2, N_CGROUPS, SUBLANES, LANES)

    part = lambda j: pl.BlockSpec((2, 1, R, LANES), lambda g, p: (p, g + j * N_CGROUPS, 0, 0))
    const = lambda a: pl.BlockSpec(a.shape, lambda g, p: (0,) * a.ndim)
    mats = (f1, f1k, f2f, f2i, f3, tc, ts)
    return pl.pallas_call(
        functools.partial(_hyena_kernel, n1=n1),
        grid=(N_CGROUPS, B // 2),
        in_specs=[part(0), part(1), part(2),
                  pl.BlockSpec((2, 1, R, LANES), lambda g, p: (p, g, 0, 0)),
                  pl.BlockSpec((2, 1, 2 * R, LANES), lambda g, p: (0, g, 0, 0)),
                  pl.BlockSpec((3, 3, None, SUBLANES, LANES), lambda g, p: (0, 0, g, 0, 0)),
                  pl.BlockSpec((3, None, SUBLANES, LANES), lambda g, p: (0, g, 0, 0)),
                  pl.BlockSpec((2, None, SUBLANES, LANES), lambda g, p: (0, g, 0, 0))]
                 + [const(m) for m in mats],
        out_specs=pl.BlockSpec((2, 1, R, LANES), lambda g, p: (p, g, 0, 0)),
        out_shape=jax.ShapeDtypeStruct((B, N_CGROUPS, R, LANES), F32),
        scratch_shapes=[pltpu.VMEM((2, SUBLANES, n1, n2), F32), pltpu.VMEM((2, SUBLANES, n1, n2), F32)],
        compiler_params=_params("parallel", "arbitrary"),
        name="hyena",
    )(u_ct, u_ct, u_ct, gate_ct, kt_ct, sw, sb, dd, *mats)


def _out_kernel(o1_ref, o4_ref, o16_ref, l1_ref, l4_ref, l16_ref, ga_ref, hg_ref, sa_ref, sh_ref, x_ref, mod_ref,
                fg_ref, e_ref, wpa_ref, wph_ref, wo_ref, out_ref, slab_ref):
    tm = x_ref.shape[1]

    def token_order(ref, r):
        if r == 1:
            return ref[0].astype(F32)
        nslab = ref.shape[-1] // LANES
        for rho in range(r):
            for t in range(nslab):
                slab_ref.at[t][pl.ds(rho, tm // r, stride=r), :] = ref[rho, :, t * LANES:(t + 1) * LANES].astype(F32)
        return jnp.concatenate([slab_ref[t] for t in range(nslab)], axis=1)

    l1, l4, l16 = (token_order(ref, r) for ref, r in zip((l1_ref, l4_ref, l16_ref), DILATIONS))
    mx = jnp.maximum(jnp.maximum(l1, l4), l16)
    e1, e4, e16 = jnp.exp(l1 - mx), jnp.exp(l4 - mx), jnp.exp(l16 - mx)
    tot = e1 + e4 + e16
    heads = lambda a: _dot((a / tot).astype(BF16), e_ref[...])
    oa = heads(e1) * token_order(o1_ref, 1)
    oa = oa + heads(e4) * token_order(o4_ref, 4)
    oa = oa + heads(e16) * token_order(o16_ref, 16)
    ya = _dot((oa * ga_ref[0].astype(F32)).astype(BF16), wpa_ref[...])
    yh = jnp.concatenate(
        [lax.dot_general(hg_ref[0, :, j * SUBLANES:(j + 1) * SUBLANES, :].reshape(D_HYENA, LANES).astype(BF16),
                         wph_ref[...], TN_DIMS, preferred_element_type=F32) for j in range(tm // LANES)], axis=0)
    mixed = sa_ref[0].astype(F32) * ya + sh_ref[0].astype(F32) * yh
    y = x_ref[0] + mod_ref[0, 2:3, :] * _dot(mixed.astype(BF16), wo_ref[...])
    ms = jnp.mean(y * y, axis=-1, keepdims=True)
    out_ref[0] = y * lax.rsqrt(ms + EPS) * fg_ref[...]


def _out_proj(o_pat, lse_pat, ga, hg_ct, sa, sh, x, mod, final_g, wpa, wph, wo, tm):
    B, L, _ = x.shape
    row = lambda width: pl.BlockSpec((1, tm, width), lambda b, i: (b, i, 0))
    strided = lambda r, width: pl.BlockSpec((None, r, tm // r, width), lambda b, i: (b, 0, i, 0))
    const = lambda shape: pl.BlockSpec(shape, lambda b, i: (0,) * len(shape))
    expand = np.zeros((LANES, D_ATTN), np.float32)
    expand[np.arange(D_ATTN) // HEAD_DIM, np.arange(D_ATTN)] = 1.0
    return pl.pallas_call(
        _out_kernel,
        grid=(B, L // tm),
        in_specs=[strided(r, D_ATTN) for r in DILATIONS] + [strided(r, LANES) for r in DILATIONS]
                 + [row(D_ATTN),
                    pl.BlockSpec((1, N_CGROUPS, tm // LANES * SUBLANES, LANES), lambda b, i: (b, 0, i, 0)),
                    row(D_MODEL), row(D_MODEL), row(D_MODEL),
                    pl.BlockSpec((1, 3, D_MODEL), lambda b, i: (b, 0, 0)),
                    const((1, D_MODEL)), const((LANES, D_ATTN)),
                    const((D_ATTN, D_MODEL)), const((D_HYENA, D_MODEL)), const((D_MODEL, D_MODEL))],
        out_specs=row(D_MODEL),
        out_shape=jax.ShapeDtypeStruct((B, L, D_MODEL), F32),
        scratch_shapes=[pltpu.VMEM((D_ATTN // LANES, tm, LANES), F32)],
        compiler_params=_params("parallel", "parallel"),
        name="out_proj",
    )(*o_pat, *lse_pat, ga, hg_ct, sa, sh, x, mod, final_g.reshape(1, -1), jnp.asarray(expand, dtype=BF16),
      wpa, wph, wo)


def _trunk(x, mod, wts):
    B, L, _ = x.shape
    assert L % 512 == 0 and B % 2 == 0
    tm = 512
    *qkv, ga, sa, sh, u_ct, gh_ct = _inproj(x, mod, wts["norm_g"], wts["w_rows"], wts["w_hy_t"], tm)
    pats = [_band_attention(*qkv[3 * j:3 * j + 3], wts["rel_bias"]) for j in range(len(DILATIONS))]
    kt_ct = _filters(L, *wts["filt"], tr=512)
    hg_ct = _hyena(u_ct, gh_ct, kt_ct, wts["short_w"], wts["short_b"], wts["hyena_d"], 2 * L // LANES)
    return _out_proj([p[0] for p in pats], [p[1] for p in pats], ga, hg_ct, sa, sh, x, mod, wts["final_g"],
                     wts["w_proj_attn"], wts["w_proj_hyena"], wts["w_out"], tm)


def kernel(x_prompt, x_sample, c_prompt, c_sample, w_ada, b_ada, norm_g, w_in, short_w, short_b, filt_w1, filt_b1,
           filt_freq1, filt_w2, filt_b2, filt_freq2, filt_w3, hyena_d, w_proj_attn, w_proj_hyena, w_out, rel_bias,
           final_g):
    assert w_ada.shape[0] == 1, "single layer"
    bp, bs = c_prompt.shape[0], c_sample.shape[0]
    c_all = jnp.concatenate([c_prompt, c_sample], axis=0)
    rows = -(-(bp + bs) // SUBLANES) * SUBLANES
    c_all = jnp.pad(c_all, ((0, rows - bp - bs), (0, 0)))
    mod = _ada(c_all, w_ada[0], b_ada[0]).reshape(rows, 3, D_MODEL)
    w = w_in[0].astype(BF16)
    wts = dict(norm_g=norm_g[0],
               w_rows=jnp.concatenate([w[:, :C_U], w[:, C_MA:]], axis=1),
               w_hy_t=w[:, C_U:C_MA].T,
               short_w=short_w[0], short_b=short_b[0],
               filt=(filt_w1[0], filt_b1[0], filt_freq1[0], filt_w2[0], filt_b2[0], filt_freq2[0], filt_w3[0]),
               hyena_d=hyena_d[0], w_proj_attn=w_proj_attn[0].astype(BF16),
               w_proj_hyena=w_proj_hyena[0].astype(BF16), w_out=w_out[0].astype(BF16), rel_bias=rel_bias,
               final_g=final_g)
    y_prompt = _trunk(x_prompt, mod[:bp], wts)
    y_sample = _trunk(x_sample, mod[bp:bp + bs], wts)
    return (y_prompt, y_sample)
```

```python
import functools
import math

import numpy as np
import jax
import jax.numpy as jnp
from jax import lax
from jax.experimental import pallas as pl
from jax.experimental.pallas import tpu as pltpu

F32 = jnp.float32
BF16 = jnp.bfloat16

D_MODEL = 1024
N_HEADS = 12
HEAD_DIM = 64
D_ATTN = N_HEADS * HEAD_DIM
N_PAIRS = N_HEADS // 2
DILATIONS = (1, 4, 16)
HALF = 64
N_BUCKETS = 32
MAX_DISTANCE = 1024
D_HYENA = 768
FILTER_EMB = 33
FILTER_HIDDEN = 64
FAST_DECAY_PCT = 0.3
SLOW_DECAY_PCT = 1.5
DECAY_TARGET = 0.01
EPS = 1e-6
NEG_INF = -1e30
LANES = 128
SUBLANES = 8
BF16_ROWS = 16
VMEM_LIMIT = 56 * 1024 * 1024
N_CGROUPS = D_HYENA // SUBLANES

C_Q, C_K, C_V, C_GA = 0, 768, 1536, 2304
C_U = 3072
C_GH = C_U + 3 * D_HYENA
C_MA = C_GH + D_HYENA
C_MH = C_MA + D_MODEL
D_PROJ = C_MH + D_MODEL
R_Q, R_K, R_V, R_GA, R_MA, R_MH = 0, 768, 1536, 2304, 3072, 4096
D_ROWMAJOR = 5120
G_GA, G_MA, G_MH = 0, 768, 1792
D_GATES = 2816

NT_DIMS = (((1,), (1,)), ((), ()))
TN_DIMS = (((0,), (0,)), ((), ()))


def _params(*sem):
    return pltpu.CompilerParams(dimension_semantics=sem, vmem_limit_bytes=VMEM_LIMIT)


def _sigmoid(x):
    return 1.0 / (1.0 + jnp.exp(-x))


def _dot(a, b):
    return jnp.dot(a, b, preferred_element_type=F32)


def _to_channel_tiled(ref, lead, row0, value):
    groups = value.shape[0] // SUBLANES
    for j in range(value.shape[1] // LANES):
        rows = slice(row0 + j * SUBLANES, row0 + (j + 1) * SUBLANES)
        ref[lead + (rows, slice(None))] = (
            value[:, j * LANES:(j + 1) * LANES].reshape(groups, SUBLANES, LANES))


def _ada_kernel(c_ref, w_ref, b_ref, o_ref):
    c = c_ref[...]
    s = c * _sigmoid(c)
    o_ref[...] = jnp.dot(s, w_ref[...], preferred_element_type=F32,
                         precision=lax.Precision.HIGHEST) + b_ref[...]


def _ada(c_pad, w_ada, b_ada):
    rows = c_pad.shape[0]
    return pl.pallas_call(
        _ada_kernel,
        grid=(3,),
        in_specs=[pl.BlockSpec((rows, D_MODEL), lambda j: (0, 0)),
                  pl.BlockSpec((D_MODEL, D_MODEL), lambda j: (0, j)),
                  pl.BlockSpec((1, D_MODEL), lambda j: (0, j))],
        out_specs=pl.BlockSpec((rows, D_MODEL), lambda j: (0, j)),
        out_shape=jax.ShapeDtypeStruct((rows, 3 * D_MODEL), F32),
        compiler_params=_params("arbitrary"),
        name="ada",
    )(c_pad, w_ada, b_ada.reshape(1, -1))


def _inproj_kernel(x_ref, mod_ref, g_ref, w_ref, wt_ref, qkv1_ref, qkv4_ref, qkv16_ref, gates_ref, ct_ref, slab_ref):
    x = x_ref[0]
    tm = x.shape[0]
    ms = jnp.mean(x * x, axis=-1, keepdims=True)
    y = x * lax.rsqrt(ms + EPS) * g_ref[...]
    h = (y * (1.0 + mod_ref[0, 1:2, :]) + mod_ref[0, 0:1, :]).astype(BF16)
    silu = lambda g: g * _sigmoid(g)

    def store_strided(which, val):
        qkv1_ref[0, which, 0] = val.astype(BF16)
        for t in range(D_ATTN // LANES):
            slab_ref[t] = val[:, t * LANES:(t + 1) * LANES]
        for r, o_ref in zip(DILATIONS[1:], (qkv4_ref, qkv16_ref)):
            for rho in range(r):
                rows = [slab_ref.at[t][pl.ds(rho, tm // r, stride=r), :] for t in range(D_ATTN // LANES)]
                o_ref[0, which, rho] = jnp.concatenate(rows, axis=1).astype(BF16)

    def gate_cols(c0, fn):
        def store(val):
            gates_ref[0, :, c0:c0 + val.shape[1]] = fn(val).astype(BF16)
        return store

    def channel_tiled(j, fn):
        return lambda val: _to_channel_tiled(ct_ref, (0, slice(j * N_CGROUPS, (j + 1) * N_CGROUPS)), 0, fn(val))

    rows = lambda c0, width: lambda: _dot(h, w_ref[:, c0:c0 + width])
    cols = lambda j: lambda: lax.dot_general(wt_ref[j * D_HYENA:(j + 1) * D_HYENA, :], h, NT_DIMS,
                                             preferred_element_type=F32)
    ident = lambda v: v
    stages = [(rows(R_Q, D_ATTN), lambda v: store_strided(0, v * (1.0 / math.sqrt(HEAD_DIM)))),
              (rows(R_K, D_ATTN), lambda v: store_strided(1, v)),
              (rows(R_V, D_ATTN), lambda v: store_strided(2, v)),
              (rows(R_GA, D_ATTN), gate_cols(G_GA, silu)),
              (rows(R_MA, D_MODEL), gate_cols(G_MA, _sigmoid)),
              (rows(R_MH, D_MODEL), gate_cols(G_MH, _sigmoid)),
              (cols(0), channel_tiled(0, ident)), (cols(1), channel_tiled(1, ident)),
              (cols(2), channel_tiled(2, ident)), (cols(3), channel_tiled(3, silu))]
    pending = stages[0][0]()
    for j, (_, finish) in enumerate(stages):
        nxt = stages[j + 1][0]() if j + 1 < len(stages) else None
        finish(pending)
        pending = nxt


def _inproj(x, mod, norm_g, w_rows, w_hy_t, tm):
    B, L, _ = x.shape
    rows_ct = tm // LANES * SUBLANES
    strided = lambda r: pl.BlockSpec((1, 3, r, tm // r, D_ATTN), lambda b, i: (b, 0, 0, i, 0))
    strided_shape = lambda r: jax.ShapeDtypeStruct((B, 3, r, L // r, D_ATTN), BF16)
    resident = lambda shape: pl.BlockSpec(shape, lambda b, i: (0, 0), pipeline_mode=pl.Buffered(1))
    return pl.pallas_call(
        _inproj_kernel,
        grid=(B, L // tm),
        in_specs=[pl.BlockSpec((1, tm, D_MODEL), lambda b, i: (b, i, 0)),
                  pl.BlockSpec((1, 3, D_MODEL), lambda b, i: (b, 0, 0)),
                  pl.BlockSpec((1, D_MODEL), lambda b, i: (0, 0)),
                  resident((D_MODEL, D_ROWMAJOR)), resident((4 * D_HYENA, D_MODEL))],
        out_specs=[strided(r) for r in DILATIONS]
                  + [pl.BlockSpec((1, tm, D_GATES), lambda b, i: (b, i, 0)),
                     pl.BlockSpec((1, 4 * N_CGROUPS, rows_ct, LANES), lambda b, i: (b, 0, i, 0))],
        out_shape=[strided_shape(r) for r in DILATIONS]
                  + [jax.ShapeDtypeStruct((B, L, D_GATES), BF16),
                     jax.ShapeDtypeStruct((B, 4 * N_CGROUPS, L // LANES * SUBLANES, LANES), F32)],
        scratch_shapes=[pltpu.VMEM((D_ATTN // LANES, tm, LANES), F32)],
        compiler_params=_params("parallel", "parallel"),
        name="inproj",
    )(x, mod, norm_g.reshape(1, -1), w_rows, w_hy_t)


def _t5_bucket(rel):
    nb = N_BUCKETS // 2
    max_exact = nb // 2
    n = np.abs(rel)
    large = max_exact + (np.log(np.maximum(n, 1) / max_exact) / math.log(MAX_DISTANCE / max_exact)
                         * (nb - max_exact)).astype(np.int32)
    large = np.minimum(large, nb - 1)
    return ((rel > 0).astype(np.int32) * nb + np.where(n < max_exact, n, large)).astype(np.int32)


def _bias_table(rel_bias, r, qb):
    a = np.arange(qb)[:, None]
    b = np.arange(qb + 2 * HALF)[None, :]
    rel = b - HALF - a
    valid = np.abs(rel) <= HALF
    onehot = (jnp.asarray(_t5_bucket(rel * r))[..., None] == jnp.arange(N_BUCKETS)).astype(F32)
    bias = jnp.dot(onehot, rel_bias.astype(F32), precision=lax.Precision.HIGHEST)
    bias = jnp.where(valid[..., None], bias, NEG_INF).transpose(2, 0, 1)
    return bias.reshape(N_PAIRS, 2 * qb, qb + 2 * HALF)


def _attn_kernel(q_ref, kp_ref, kc_ref, kn_ref, vp_ref, vc_ref, vn_ref, bias_ref, o_ref, lse_ref, *, mb, qb, lr):
    i = pl.program_id(1)
    nsub = mb // qb
    kw = qb + 2 * HALF
    lane = lax.broadcasted_iota(jnp.int32, (qb, LANES), 1)
    lo = lane < HEAD_DIM
    kcol = lax.broadcasted_iota(jnp.int32, (1, kw), 1)

    def window(p_ref, c_ref, n_ref, s, cols):
        parts = []
        if s == 0:
            parts.append(p_ref[:, cols])
        lo_row = max(s * qb - HALF, 0)
        hi_row = min((s + 1) * qb + HALF, mb)
        parts.append(c_ref[lo_row:hi_row, cols])
        if s == nsub - 1:
            parts.append(n_ref[:, cols])
        return parts[0] if len(parts) == 1 else jnp.concatenate(parts, axis=0)

    for s in range(nsub):
        edge = None
        if s == 0 or s == nsub - 1:
            pos = kcol + (i * mb + (s * qb - HALF))
            edge = jnp.where(pos >= 0, jnp.where(pos < lr, 0.0, NEG_INF), NEG_INF)
        lse_tile = jnp.zeros((qb, LANES), F32)
        for p in range(N_PAIRS):
            cols = slice(p * LANES, (p + 1) * LANES)
            qp = q_ref[s * qb:(s + 1) * qb, cols]
            zero = jnp.zeros_like(qp)
            q2 = jnp.concatenate([jnp.where(lo, qp, zero), jnp.where(lo, zero, qp)], axis=0)
            kwin = window(kp_ref, kc_ref, kn_ref, s, cols)
            vwin = window(vp_ref, vc_ref, vn_ref, s, cols)
            logits = lax.dot_general(q2, kwin, NT_DIMS, preferred_element_type=F32)
            logits = logits + bias_ref[p]
            if edge is not None:
                logits = logits + edge
            m = jnp.max(logits, axis=-1, keepdims=True)
            e = jnp.exp(logits - m)
            l = jnp.sum(e, axis=-1, keepdims=True)
            o = _dot(e.astype(BF16), vwin) / l
            o_ref[s * qb:(s + 1) * qb, cols] = jnp.where(lo, o[:qb], o[qb:]).astype(BF16)
            lse = m + jnp.log(l)
            lse_tile = jnp.where(lane == 2 * p, lse[:qb], lse_tile)
            lse_tile = jnp.where(lane == 2 * p + 1, lse[qb:], lse_tile)
        lse_ref[s * qb:(s + 1) * qb, :] = lse_tile


def _band_attention(qkv, rel_bias):
    B, _, r, lr, _ = qkv.shape
    qb = 128
    mb = min(512, lr)
    hb = mb // HALF
    nhalo = lr // HALF
    cur_rows = lambda i: i
    prev_rows = lambda i: jnp.maximum(i * hb - 1, 0)
    next_rows = lambda i: jnp.minimum((i + 1) * hb, nhalo - 1)
    part = lambda which, rows, idx: pl.BlockSpec((None, None, None, rows, D_ATTN),
                                                 lambda b, i, rho: (b, which, rho, idx(i), 0))
    out = lambda width: pl.BlockSpec((None, None, mb, width), lambda b, i, rho: (b, rho, i, 0))
    bias = _bias_table(rel_bias, r, qb)
    return pl.pallas_call(
        functools.partial(_attn_kernel, mb=mb, qb=qb, lr=lr),
        grid=(B, lr // mb, r),
        in_specs=[part(0, mb, cur_rows),
                  part(1, HALF, prev_rows), part(1, mb, cur_rows), part(1, HALF, next_rows),
                  part(2, HALF, prev_rows), part(2, mb, cur_rows), part(2, HALF, next_rows),
                  pl.BlockSpec(bias.shape, lambda b, i, rho: (0, 0, 0))],
        out_specs=[out(D_ATTN), out(LANES)],
        out_shape=[jax.ShapeDtypeStruct((B, r, lr, D_ATTN), BF16),
                   jax.ShapeDtypeStruct((B, r, lr, LANES), F32)],
        compiler_params=_params("parallel", "parallel", "parallel"),
        name=f"attn_r{r}",
    )(*([qkv] * 7), bias)


def _filter_features(L):
    j = jnp.arange(2 * L, dtype=jnp.int32)
    pos = jnp.where(j < L, j, 2 * L - j).astype(F32)
    t = (pos / (L - 1))[:, None]
    bands = (FILTER_EMB - 1) // 2
    w = (2.0 * math.pi / L) * pos[:, None]
    f = jnp.linspace(1e-4, bands - 1, bands, dtype=F32)[None, :]
    z = jnp.concatenate([t, jnp.cos(f * w), -jnp.sin(f * w)], axis=-1)
    return jnp.pad(z, ((0, 0), (0, LANES - FILTER_EMB)))


def _filt_kernel(z_ref, w1_ref, b1_ref, f1_ref, w2_ref, b2_ref, f2_ref, w3a_ref, w3b_ref, w3a0_ref, w3b0_ref,
                 dl_ref, o_ref, *, tr, L):
    i = pl.program_id(0)
    h = jnp.sin(f1_ref[...] * (_dot(z_ref[...], w1_ref[...]) + b1_ref[...]))
    h = jnp.sin(f2_ref[...] * (_dot(h, w2_ref[...]) + b2_ref[...]))
    lane = lax.broadcasted_iota(jnp.int32, (1, LANES), 1)
    for n, (w_ref, w0_ref) in enumerate(((w3a_ref, w3a0_ref), (w3b_ref, w3b0_ref))):
        kt = lax.dot_general(w_ref[...], h, NT_DIMS, preferred_element_type=F32)
        for j in range(tr // LANES):
            col = i * tr + j * LANES + lane
            pos = jnp.where(col < L, col, 2 * L - col).astype(F32)
            decay = jnp.exp(-(pos / (L - 1)) * dl_ref[...])
            slab = jnp.where(col != L, kt[:, j * LANES:(j + 1) * LANES] * decay, 0.0)
            o_ref[n, :, j * SUBLANES:(j + 1) * SUBLANES, :] = slab.reshape(N_CGROUPS, SUBLANES, LANES)

        @pl.when(i == 0)
        def _():
            back = lax.dot_general(w0_ref[...], h[0:LANES], NT_DIMS, preferred_element_type=F32)
            add = jnp.where(lane == 0, back, 0.0).reshape(N_CGROUPS, SUBLANES, LANES)
            o_ref[n, :, 0:SUBLANES, :] = o_ref[n, :, 0:SUBLANES, :] + add


def _filters(L, w1, b1, f1, w2, b2, f2, w3, tr):
    pad_h = LANES - FILTER_HIDDEN
    z = _filter_features(L)
    w1p = jnp.pad(w1, ((0, LANES - FILTER_EMB), (0, pad_h)))
    w2p = jnp.pad(w2, ((0, pad_h), (0, pad_h)))
    w3t = jnp.pad(w3, ((0, pad_h), (0, 0))).T
    vec = lambda a: jnp.pad(a.reshape(1, -1), ((0, 0), (0, pad_h)))
    deltas = np.abs(np.linspace(math.log(DECAY_TARGET) / SLOW_DECAY_PCT, math.log(DECAY_TARGET) / FAST_DECAY_PCT,
                                D_HYENA)).astype(np.float32)
    deltas = jnp.asarray(np.broadcast_to(deltas[:, None], (D_HYENA, LANES)))
    nt = 2 * L // tr
    half = nt // 2
    const = lambda shape: pl.BlockSpec(shape, lambda i: (0, 0))
    w3spec = lambda n: pl.BlockSpec((D_HYENA, LANES), lambda i: (2 * n + jnp.where(i < half, 0, 1), 0))
    w3back = lambda n: pl.BlockSpec((D_HYENA, LANES), lambda i: (2 * n + 1, 0))
    rows_ct = tr // LANES * SUBLANES
    return pl.pallas_call(
        functools.partial(_filt_kernel, tr=tr, L=L),
        grid=(nt,),
        in_specs=[pl.BlockSpec((tr, LANES), lambda i: (i, 0)),
                  const((LANES, LANES)), const((1, LANES)), const((1, LANES)),
                  const((LANES, LANES)), const((1, LANES)), const((1, LANES)),
                  w3spec(0), w3spec(1), w3back(0), w3back(1), const((D_HYENA, LANES))],
        out_specs=pl.BlockSpec((2, N_CGROUPS, rows_ct, LANES), lambda i: (0, 0, i, 0)),
        out_shape=jax.ShapeDtypeStruct((2, N_CGROUPS, 2 * L // LANES * SUBLANES, LANES), F32),
        compiler_params=_params("arbitrary"),
        name="hyena_filters",
    )(z, w1p, vec(b1), vec(f1), w2p, vec(b2), vec(f2), w3t, w3t, w3t, w3t, deltas)


def _dft_left(n_out, n_in, n, sign):
    a = np.arange(n_out)[:, None]
    b = np.arange(n_in)[None, :]
    ang = 2.0 * np.pi * ((a * b) % n) / n
    c, s = np.cos(ang), sign * np.sin(ang)
    return np.block([[c, -s], [s, c]])


def _dft_right(n, sign):
    return _dft_left(n, n, n, sign).T


def _hyena_kernel(v_ref, x1_ref, x2_ref, gate_ref, kt_ref, sw_ref, sb_ref, d_ref, f1_ref, f1k_ref, f2f_ref, f2i_ref,
                  f3_ref, tc_ref, ts_ref, o_ref, kr_ref, ki_ref, *, n1):
    n1h = n1 // 2
    n2 = LANES
    nch = SUBLANES
    tc, ts = tc_ref[...], ts_ref[...]
    ts_neg = -ts

    def channel_rows(ref2d, c, n):
        return ref2d[pl.ds(c, n, stride=SUBLANES), :]

    def lanes(mats):
        return jnp.concatenate(mats, axis=1)

    def cmul(ar, ai, br, bi):
        return lanes([ar * br - ai * bi, ar * bi + ai * br])

    def channel_parts(a, c, rows):
        return a[:rows, c * n2:(c + 1) * n2], a[rows:, c * n2:(c + 1) * n2]

    def level2(lhs, mat_ref):
        x = _dot(lhs, mat_ref[...])
        return x[:, :n2], x[:, n2:]

    @pl.when(pl.program_id(1) == 0)
    def _():
        for n in range(2):
            kmat = lanes([channel_rows(kt_ref.at[n, 0], c, n1) for c in range(nch)]).astype(BF16)
            a = _dot(f1k_ref[...], kmat).astype(BF16)
            for c in range(nch):
                ar, ai = channel_parts(a, c, n1)
                xr, xi = level2(cmul(ar, ai, tc, ts_neg), f2f_ref)
                kr_ref[n, c] = (xr * (1.0 / (n1 * n2))).astype(BF16)
                ki_ref[n, c] = (xi * (1.0 / (n1 * n2))).astype(BF16)

    lane = lax.broadcasted_iota(jnp.int32, (n1h, n2), 1)
    row = lax.broadcasted_iota(jnp.int32, (n1h, n2), 0)

    def short_conv(ref, part, b, c):
        m = channel_rows(ref.at[b, 0], c, n1h)
        left = pltpu.roll(m, 1, 1)
        before = jnp.where(lane == 0, jnp.where(row == 0, 0.0, pltpu.roll(left, 1, 0)), left)
        right = pltpu.roll(m, n2 - 1, 1)
        after = jnp.where(lane == n2 - 1, jnp.where(row == n1h - 1, 0.0, pltpu.roll(right, n1h - 1, 0)), right)
        tap = lambda k: sw_ref[k, part, c:c + 1, :]
        return before * tap(0) + m * tap(1) + after * tap(2) + sb_ref[part, c:c + 1, :]

    z = [[short_conv(v_ref, 0, b, c) for c in range(nch)] for b in range(2)]
    for n in range(2):
        stacked = jnp.concatenate([lanes(z[0]), lanes(z[1])], axis=0).astype(BF16)
        a = _dot(f1_ref[...], stacked).astype(BF16)
        brs, bis = [], []
        for c in range(nch):
            ar, ai = channel_parts(a, c, n1)
            xr, xi = level2(cmul(ar, ai, tc, ts_neg), f2f_ref)
            br, bi = level2(cmul(xr.astype(BF16), xi.astype(BF16), kr_ref[n, c], ki_ref[n, c]), f2i_ref)
            back = cmul(br.astype(BF16), bi.astype(BF16), tc, ts)
            brs.append(back[:, :n2])
            bis.append(back[:, n2:])
        y = _dot(f3_ref[...], jnp.concatenate([lanes(brs), lanes(bis)], axis=0))
        x_ref = x1_ref if n == 0 else x2_ref
        for b in range(2):
            for c in range(nch):
                conv = y[b * n1h:(b + 1) * n1h, c * n2:(c + 1) * n2] + z[b][c] * d_ref[n, c:c + 1, :]
                nxt = short_conv(x_ref, n + 1, b, c) * conv
                if n == 0:
                    z[b][c] = nxt
                else:
                    gate = channel_rows(gate_ref.at[b, 0], c, n1h)
                    o_ref.at[b, 0][pl.ds(c, n1h, stride=SUBLANES), :] = nxt * gate


def _hyena(u_ct, kt_ct, short_w, short_b, hyena_d, n1):
    B, _, R, _ = u_ct.shape
    n1h, n2 = n1 // 2, LANES
    bf = lambda m: jnp.asarray(m, dtype=BF16)
    f1 = bf(_dft_left(n1, n1h, n1, -1))
    f1k = bf(_dft_left(n1, n1, n1, -1)[:, :n1])
    f2f, f2i = bf(_dft_right(n2, -1)), bf(_dft_right(n2, +1))
    f3 = bf(_dft_left(n1h, n1, n1, +1))
    ang = 2.0 * np.pi * ((np.arange(n1)[:, None] * np.arange(n2)[None, :]) % (n1 * n2)) / (n1 * n2)
    tc, ts = bf(np.cos(ang)), bf(np.sin(ang))
    wide = lambda t: jnp.broadcast_to(t[..., None], t.shape + (LANES,))
    sw = wide(short_w).reshape(3, 3, N_CGROUPS, SUBLANES, LANES)
    sb = wide(short_b).reshape(3, N_CGROUPS, SUBLANES, LANES)
    dd = wide(hyena_d).reshape(2, N_CGROUPS, SUBLANES, LANES)

    part = lambda j: pl.BlockSpec((2, 1, R, LANES), lambda g, p: (p, g + j * N_CGROUPS, 0, 0))
    const = lambda a: pl.BlockSpec(a.shape, lambda g, p: (0,) * a.ndim)
    mats = (f1, f1k, f2f, f2i, f3, tc, ts)
    return pl.pallas_call(
        functools.partial(_hyena_kernel, n1=n1),
        grid=(N_CGROUPS, B // 2),
        in_specs=[part(0), part(1), part(2), part(3),
                  pl.BlockSpec((2, 1, 2 * R, LANES), lambda g, p: (0, g, 0, 0)),
                  pl.BlockSpec((3, 3, None, SUBLANES, LANES), lambda g, p: (0, 0, g, 0, 0)),
                  pl.BlockSpec((3, None, SUBLANES, LANES), lambda g, p: (0, g, 0, 0)),
                  pl.BlockSpec((2, None, SUBLANES, LANES), lambda g, p: (0, g, 0, 0))]
                 + [const(m) for m in mats],
        out_specs=pl.BlockSpec((2, 1, R, LANES), lambda g, p: (p, g, 0, 0)),
        out_shape=jax.ShapeDtypeStruct((B, N_CGROUPS, R, LANES), F32),
        scratch_shapes=[pltpu.VMEM((2, SUBLANES, n1, n2), BF16), pltpu.VMEM((2, SUBLANES, n1, n2), BF16)],
        compiler_params=_params("parallel", "arbitrary"),
        name="hyena",
    )(u_ct, u_ct, u_ct, u_ct, kt_ct, sw, sb, dd, *mats)


def _out_kernel(o1_ref, o4_ref, o16_ref, l1_ref, l4_ref, l16_ref, gates_ref, hg_ref, x_ref, mod_ref,
                fg_ref, e_ref, wpa_ref, wph_ref, wo_ref, out_ref, slab_ref):
    tm = x_ref.shape[1]

    def token_order(ref, r):
        if r == 1:
            return ref[0].astype(F32)
        nslab = ref.shape[-1] // LANES
        for rho in range(r):
            for t in range(nslab):
                slab_ref.at[t][pl.ds(rho, tm // r, stride=r), :] = ref[rho, :, t * LANES:(t + 1) * LANES].astype(F32)
        return jnp.concatenate([slab_ref[t] for t in range(nslab)], axis=1)

    l1, l4, l16 = (token_order(ref, r) for ref, r in zip((l1_ref, l4_ref, l16_ref), DILATIONS))
    mx = jnp.maximum(jnp.maximum(l1, l4), l16)
    e1, e4, e16 = jnp.exp(l1 - mx), jnp.exp(l4 - mx), jnp.exp(l16 - mx)
    tot = e1 + e4 + e16
    heads = lambda a: _dot((a / tot).astype(BF16), e_ref[...])
    oa = heads(e1) * token_order(o1_ref, 1)
    oa = oa + heads(e4) * token_order(o4_ref, 4)
    oa = oa + heads(e16) * token_order(o16_ref, 16)
    gate = lambda c0, width: gates_ref[0, :, c0:c0 + width].astype(F32)
    ya = _dot((oa * gate(G_GA, D_ATTN)).astype(BF16), wpa_ref[...])
    yh = jnp.concatenate(
        [lax.dot_general(hg_ref[0, :, j * SUBLANES:(j + 1) * SUBLANES, :].reshape(D_HYENA, LANES).astype(BF16),
                         wph_ref[...], TN_DIMS, preferred_element_type=F32) for j in range(tm // LANES)], axis=0)
    mixed = gate(G_MA, D_MODEL) * ya + gate(G_MH, D_MODEL) * yh
    y = x_ref[0] + mod_ref[0, 2:3, :] * _dot(mixed.astype(BF16), wo_ref[...])
    ms = jnp.mean(y * y, axis=-1, keepdims=True)
    out_ref[0] = y * lax.rsqrt(ms + EPS) * fg_ref[...]


def _out_proj(o_pat, lse_pat, gates, hg_ct, x, mod, final_g, wpa, wph, wo, tm):
    B, L, _ = x.shape
    row = lambda width: pl.BlockSpec((1, tm, width), lambda b, i: (b, i, 0))
    strided = lambda r, width: pl.BlockSpec((None, r, tm // r, width), lambda b, i: (b, 0, i, 0))
    const = lambda shape: pl.BlockSpec(shape, lambda b, i: (0,) * len(shape))
    expand = np.zeros((LANES, D_ATTN), np.float32)
    expand[np.arange(D_ATTN) // HEAD_DIM, np.arange(D_ATTN)] = 1.0
    return pl.pallas_call(
        _out_kernel,
        grid=(B, L // tm),
        in_specs=[strided(r, D_ATTN) for r in DILATIONS] + [strided(r, LANES) for r in DILATIONS]
                 + [row(D_GATES),
                    pl.BlockSpec((1, N_CGROUPS, tm // LANES * SUBLANES, LANES), lambda b, i: (b, 0, i, 0)),
                    row(D_MODEL),
                    pl.BlockSpec((1, 3, D_MODEL), lambda b, i: (b, 0, 0)),
                    const((1, D_MODEL)), const((LANES, D_ATTN)),
                    const((D_ATTN, D_MODEL)), const((D_HYENA, D_MODEL)), const((D_MODEL, D_MODEL))],
        out_specs=row(D_MODEL),
        out_shape=jax.ShapeDtypeStruct((B, L, D_MODEL), F32),
        scratch_shapes=[pltpu.VMEM((D_ATTN // LANES, tm, LANES), F32)],
        compiler_params=_params("parallel", "parallel"),
        name="out_proj",
    )(*o_pat, *lse_pat, gates, hg_ct, x, mod, final_g.reshape(1, -1), jnp.asarray(expand, dtype=BF16),
      wpa, wph, wo)


def _trunk(x, mod, wts):
    B, L, _ = x.shape
    assert L % 512 == 0 and B % 2 == 0
    tm = 512
    *qkv, gates, u_ct = _inproj(x, mod, wts["norm_g"], wts["w_rows"], wts["w_hy_t"], tm)
    pats = [_band_attention(t, wts["rel_bias"]) for t in qkv]
    kt_ct = _filters(L, *wts["filt"], tr=512)
    hg_ct = _hyena(u_ct, kt_ct, wts["short_w"], wts["short_b"], wts["hyena_d"], 2 * L // LANES)
    return _out_proj([p[0] for p in pats], [p[1] for p in pats], gates, hg_ct, x, mod, wts["final_g"],
                     wts["w_proj_attn"], wts["w_proj_hyena"], wts["w_out"], tm)


def kernel(x_prompt, x_sample, c_prompt, c_sample, w_ada, b_ada, norm_g, w_in, short_w, short_b, filt_w1, filt_b1,
           filt_freq1, filt_w2, filt_b2, filt_freq2, filt_w3, hyena_d, w_proj_attn, w_proj_hyena, w_out, rel_bias,
           final_g):
    assert w_ada.shape[0] == 1, "single layer"
    bp, bs = c_prompt.shape[0], c_sample.shape[0]
    c_all = jnp.concatenate([c_prompt, c_sample], axis=0)
    rows = -(-(bp + bs) // SUBLANES) * SUBLANES
    c_all = jnp.pad(c_all, ((0, rows - bp - bs), (0, 0)))
    mod = _ada(c_all, w_ada[0], b_ada[0]).reshape(rows, 3, D_MODEL)
    w = w_in[0].astype(BF16)
    wts = dict(norm_g=norm_g[0],
               w_rows=jnp.concatenate([w[:, :C_U], w[:, C_MA:]], axis=1),
               w_hy_t=w[:, C_U:C_MA].T,
               short_w=short_w[0], short_b=short_b[0],
               filt=(filt_w1[0], filt_b1[0], filt_freq1[0], filt_w2[0], filt_b2[0], filt_freq2[0], filt_w3[0]),
               hyena_d=hyena_d[0], w_proj_attn=w_proj_attn[0].astype(BF16),
               w_proj_hyena=w_proj_hyena[0].astype(BF16), w_out=w_out[0].astype(BF16), rel_bias=rel_bias,
               final_g=final_g)
    y_prompt = _trunk(x_prompt, mod[:bp], wts)
    y_sample = _trunk(x_sample, mod[bp:bp + bs], wts)
    return (y_prompt, y_sample)
```

```python
import functools
import math

import numpy as np
import jax
import jax.numpy as jnp
from jax import lax
from jax.experimental import pallas as pl
from jax.experimental.pallas import tpu as pltpu

F32 = jnp.float32
BF16 = jnp.bfloat16

D_MODEL = 1024
N_HEADS = 12
HEAD_DIM = 64
D_ATTN = N_HEADS * HEAD_DIM
N_PAIRS = N_HEADS // 2
DILATIONS = (1, 4, 16)
HALF = 64
N_BUCKETS = 32
MAX_DISTANCE = 1024
D_HYENA = 768
FILTER_EMB = 33
FILTER_HIDDEN = 64
FAST_DECAY_PCT = 0.3
SLOW_DECAY_PCT = 1.5
DECAY_TARGET = 0.01
EPS = 1e-6
NEG_INF = -1e30
LANES = 128
SUBLANES = 8
BF16_ROWS = 16
VMEM_LIMIT = 56 * 1024 * 1024
N_CGROUPS = D_HYENA // SUBLANES

C_Q, C_K, C_V, C_GA = 0, 768, 1536, 2304
C_U = 3072
C_GH = C_U + 3 * D_HYENA
C_MA = C_GH + D_HYENA
C_MH = C_MA + D_MODEL
D_PROJ = C_MH + D_MODEL
R_Q, R_K, R_V, R_GA, R_MA, R_MH = 0, 768, 1536, 2304, 3072, 4096
D_ROWMAJOR = 5120
G_GA, G_MA, G_MH = 0, 768, 1792
D_GATES = 2816

NT_DIMS = (((1,), (1,)), ((), ()))
TN_DIMS = (((0,), (0,)), ((), ()))


def _params(*sem):
    return pltpu.CompilerParams(dimension_semantics=sem, vmem_limit_bytes=VMEM_LIMIT)


def _sigmoid(x):
    return 1.0 / (1.0 + jnp.exp(-x))


def _dot(a, b):
    return jnp.dot(a, b, preferred_element_type=F32)


def _to_channel_tiled(ref, lead, value):
    groups = value.shape[0] // SUBLANES
    band = lambda j: value[:, j * LANES:(j + 1) * LANES].reshape(groups, SUBLANES, LANES)
    for j in range(0, value.shape[1] // LANES, 2):
        rows = slice(j * SUBLANES, (j + 2) * SUBLANES)
        ref[lead + (rows, slice(None))] = jnp.concatenate([band(j), band(j + 1)], axis=1).astype(ref.dtype)


def _ada_kernel(c_ref, w_ref, b_ref, o_ref):
    c = c_ref[...]
    s = c * _sigmoid(c)
    o_ref[...] = jnp.dot(s, w_ref[...], preferred_element_type=F32,
                         precision=lax.Precision.HIGHEST) + b_ref[...]


def _ada(c_pad, w_ada, b_ada):
    rows = c_pad.shape[0]
    return pl.pallas_call(
        _ada_kernel,
        grid=(3,),
        in_specs=[pl.BlockSpec((rows, D_MODEL), lambda j: (0, 0)),
                  pl.BlockSpec((D_MODEL, D_MODEL), lambda j: (0, j)),
                  pl.BlockSpec((1, D_MODEL), lambda j: (0, j))],
        out_specs=pl.BlockSpec((rows, D_MODEL), lambda j: (0, j)),
        out_shape=jax.ShapeDtypeStruct((rows, 3 * D_MODEL), F32),
        compiler_params=_params("arbitrary"),
        name="ada",
    )(c_pad, w_ada, b_ada.reshape(1, -1))


def _inproj_kernel(x_ref, mod_ref, g_ref, w_ref, wt_ref, qkv1_ref, qkv4_ref, qkv16_ref, gates_ref, ct_ref, slab_ref):
    x = x_ref[0]
    tm = x.shape[0]
    ms = jnp.mean(x * x, axis=-1, keepdims=True)
    y = x * lax.rsqrt(ms + EPS) * g_ref[...]
    h = (y * (1.0 + mod_ref[0, 1:2, :]) + mod_ref[0, 0:1, :]).astype(BF16)
    silu = lambda g: g * _sigmoid(g)

    def store_strided(which, val):
        qkv1_ref[0, which, 0] = val.astype(BF16)
        for t in range(D_ATTN // LANES):
            slab_ref[t] = val[:, t * LANES:(t + 1) * LANES]
        for r, o_ref in zip(DILATIONS[1:], (qkv4_ref, qkv16_ref)):
            for rho in range(r):
                rows = [slab_ref.at[t][pl.ds(rho, tm // r, stride=r), :] for t in range(D_ATTN // LANES)]
                o_ref[0, which, rho] = jnp.concatenate(rows, axis=1).astype(BF16)

    def gate_cols(c0, fn):
        def store(val):
            gates_ref[0, :, c0:c0 + val.shape[1]] = fn(val).astype(BF16)
        return store

    def channel_tiled(j, fn):
        return lambda val: _to_channel_tiled(ct_ref, (0, slice(j * N_CGROUPS, (j + 1) * N_CGROUPS)), fn(val))

    rows = lambda c0, width: lambda: _dot(h, w_ref[:, c0:c0 + width])
    cols = lambda j: lambda: lax.dot_general(wt_ref[j * D_HYENA:(j + 1) * D_HYENA, :], h, NT_DIMS,
                                             preferred_element_type=F32)
    ident = lambda v: v
    stages = [(rows(R_Q, D_ATTN), lambda v: store_strided(0, v * (1.0 / math.sqrt(HEAD_DIM)))),
              (rows(R_K, D_ATTN), lambda v: store_strided(1, v)),
              (rows(R_V, D_ATTN), lambda v: store_strided(2, v)),
              (rows(R_GA, D_ATTN), gate_cols(G_GA, silu)),
              (rows(R_MA, D_MODEL), gate_cols(G_MA, _sigmoid)),
              (rows(R_MH, D_MODEL), gate_cols(G_MH, _sigmoid)),
              (cols(0), channel_tiled(0, ident)), (cols(1), channel_tiled(1, ident)),
              (cols(2), channel_tiled(2, ident)), (cols(3), channel_tiled(3, silu))]
    pending = stages[0][0]()
    for j, (_, finish) in enumerate(stages):
        nxt = stages[j + 1][0]() if j + 1 < len(stages) else None
        finish(pending)
        pending = nxt


def _inproj(x, mod, norm_g, w_rows, w_hy_t, tm):
    B, L, _ = x.shape
    rows_ct = tm // LANES * SUBLANES
    strided = lambda r: pl.BlockSpec((1, 3, r, tm // r, D_ATTN), lambda b, i: (b, 0, 0, i, 0))
    strided_shape = lambda r: jax.ShapeDtypeStruct((B, 3, r, L // r, D_ATTN), BF16)
    resident = lambda shape: pl.BlockSpec(shape, lambda b, i: (0, 0), pipeline_mode=pl.Buffered(1))
    return pl.pallas_call(
        _inproj_kernel,
        grid=(B, L // tm),
        in_specs=[pl.BlockSpec((1, tm, D_MODEL), lambda b, i: (b, i, 0)),
                  pl.BlockSpec((1, 3, D_MODEL), lambda b, i: (b, 0, 0)),
                  pl.BlockSpec((1, D_MODEL), lambda b, i: (0, 0)),
                  resident((D_MODEL, D_ROWMAJOR)), resident((4 * D_HYENA, D_MODEL))],
        out_specs=[strided(r) for r in DILATIONS]
                  + [pl.BlockSpec((1, tm, D_GATES), lambda b, i: (b, i, 0)),
                     pl.BlockSpec((1, 4 * N_CGROUPS, rows_ct, LANES), lambda b, i: (b, 0, i, 0))],
        out_shape=[strided_shape(r) for r in DILATIONS]
                  + [jax.ShapeDtypeStruct((B, L, D_GATES), BF16),
                     jax.ShapeDtypeStruct((B, 4 * N_CGROUPS, L // LANES * SUBLANES, LANES), BF16)],
        scratch_shapes=[pltpu.VMEM((D_ATTN // LANES, tm, LANES), F32)],
        compiler_params=_params("parallel", "parallel"),
        name="inproj",
    )(x, mod, norm_g.reshape(1, -1), w_rows, w_hy_t)


def _t5_bucket(rel):
    nb = N_BUCKETS // 2
    max_exact = nb // 2
    n = np.abs(rel)
    large = max_exact + (np.log(np.maximum(n, 1) / max_exact) / math.log(MAX_DISTANCE / max_exact)
                         * (nb - max_exact)).astype(np.int32)
    large = np.minimum(large, nb - 1)
    return ((rel > 0).astype(np.int32) * nb + np.where(n < max_exact, n, large)).astype(np.int32)


def _bias_table(rel_bias, r, qb):
    a = np.arange(qb)[:, None]
    b = np.arange(qb + 2 * HALF)[None, :]
    rel = b - HALF - a
    valid = np.abs(rel) <= HALF
    onehot = (jnp.asarray(_t5_bucket(rel * r))[..., None] == jnp.arange(N_BUCKETS)).astype(F32)
    bias = jnp.dot(onehot, rel_bias.astype(F32), precision=lax.Precision.HIGHEST)
    bias = jnp.where(valid[..., None], bias, NEG_INF).transpose(2, 0, 1)
    return bias.reshape(N_PAIRS, 2 * qb, qb + 2 * HALF)


def _attn_kernel(q_ref, kp_ref, kc_ref, kn_ref, vp_ref, vc_ref, vn_ref, bias_ref, o_ref, lse_ref, *, mb, qb, lr):
    i = pl.program_id(1)
    nsub = mb // qb
    kw = qb + 2 * HALF
    lane = lax.broadcasted_iota(jnp.int32, (qb, LANES), 1)
    lo = lane < HEAD_DIM
    kcol = lax.broadcasted_iota(jnp.int32, (1, kw), 1)

    def window(p_ref, c_ref, n_ref, s, cols):
        parts = []
        if s == 0:
            parts.append(p_ref[:, cols])
        lo_row = max(s * qb - HALF, 0)
        hi_row = min((s + 1) * qb + HALF, mb)
        parts.append(c_ref[lo_row:hi_row, cols])
        if s == nsub - 1:
            parts.append(n_ref[:, cols])
        return parts[0] if len(parts) == 1 else jnp.concatenate(parts, axis=0)

    for s in range(nsub):
        edge = None
        if s == 0 or s == nsub - 1:
            pos = kcol + (i * mb + (s * qb - HALF))
            edge = jnp.where(pos >= 0, jnp.where(pos < lr, 0.0, NEG_INF), NEG_INF)
        lse_tile = jnp.zeros((qb, LANES), F32)
        for p in range(N_PAIRS):
            cols = slice(p * LANES, (p + 1) * LANES)
            qp = q_ref[s * qb:(s + 1) * qb, cols]
            zero = jnp.zeros_like(qp)
            q2 = jnp.concatenate([jnp.where(lo, qp, zero), jnp.where(lo, zero, qp)], axis=0)
            kwin = window(kp_ref, kc_ref, kn_ref, s, cols)
            vwin = window(vp_ref, vc_ref, vn_ref, s, cols)
            logits = lax.dot_general(q2, kwin, NT_DIMS, preferred_element_type=F32)
            logits = logits + bias_ref[p]
            if edge is not None:
                logits = logits + edge
            m = jnp.max(logits, axis=-1, keepdims=True)
            e = jnp.exp(logits - m)
            l = jnp.sum(e, axis=-1, keepdims=True)
            o = _dot(e.astype(BF16), vwin) / l
            o_ref[s * qb:(s + 1) * qb, cols] = jnp.where(lo, o[:qb], o[qb:]).astype(BF16)
            lse = m + jnp.log(l)
            lse_tile = jnp.where(lane == 2 * p, lse[:qb], lse_tile)
            lse_tile = jnp.where(lane == 2 * p + 1, lse[qb:], lse_tile)
        lse_ref[s * qb:(s + 1) * qb, :] = lse_tile


def _band_attention(qkv, rel_bias):
    B, _, r, lr, _ = qkv.shape
    qb = 128
    mb = min(512, lr)
    hb = mb // HALF
    nhalo = lr // HALF
    cur_rows = lambda i: i
    prev_rows = lambda i: jnp.maximum(i * hb - 1, 0)
    next_rows = lambda i: jnp.minimum((i + 1) * hb, nhalo - 1)
    part = lambda which, rows, idx: pl.BlockSpec((None, None, None, rows, D_ATTN),
                                                 lambda b, i, rho: (b, which, rho, idx(i), 0))
    out = lambda width: pl.BlockSpec((None, None, mb, width), lambda b, i, rho: (b, rho, i, 0))
    bias = _bias_table(rel_bias, r, qb)
    return pl.pallas_call(
        functools.partial(_attn_kernel, mb=mb, qb=qb, lr=lr),
        grid=(B, lr // mb, r),
        in_specs=[part(0, mb, cur_rows),
                  part(1, HALF, prev_rows), part(1, mb, cur_rows), part(1, HALF, next_rows),
                  part(2, HALF, prev_rows), part(2, mb, cur_rows), part(2, HALF, next_rows),
                  pl.BlockSpec(bias.shape, lambda b, i, rho: (0, 0, 0))],
        out_specs=[out(D_ATTN), out(LANES)],
        out_shape=[jax.ShapeDtypeStruct((B, r, lr, D_ATTN), BF16),
                   jax.ShapeDtypeStruct((B, r, lr, LANES), F32)],
        compiler_params=_params("parallel", "parallel", "parallel"),
        name=f"attn_r{r}",
    )(*([qkv] * 7), bias)


def _filter_features(L):
    j = jnp.arange(2 * L, dtype=jnp.int32)
    pos = jnp.where(j < L, j, 2 * L - j).astype(F32)
    t = (pos / (L - 1))[:, None]
    bands = (FILTER_EMB - 1) // 2
    w = (2.0 * math.pi / L) * pos[:, None]
    f = jnp.linspace(1e-4, bands - 1, bands, dtype=F32)[None, :]
    z = jnp.concatenate([t, jnp.cos(f * w), -jnp.sin(f * w)], axis=-1)
    return jnp.pad(z, ((0, 0), (0, LANES - FILTER_EMB)))


def _filt_kernel(z_ref, w1_ref, b1_ref, f1_ref, w2_ref, b2_ref, f2_ref, w3a_ref, w3b_ref, w3a0_ref, w3b0_ref,
                 dl_ref, o_ref, *, tr, L):
    i = pl.program_id(0)
    h = jnp.sin(f1_ref[...] * (_dot(z_ref[...], w1_ref[...]) + b1_ref[...]))
    h = jnp.sin(f2_ref[...] * (_dot(h, w2_ref[...]) + b2_ref[...]))
    lane = lax.broadcasted_iota(jnp.int32, (1, LANES), 1)
    for n, (w_ref, w0_ref) in enumerate(((w3a_ref, w3a0_ref), (w3b_ref, w3b0_ref))):
        kt = lax.dot_general(w_ref[...], h, NT_DIMS, preferred_element_type=F32)
        for j in range(tr // LANES):
            col = i * tr + j * LANES + lane
            pos = jnp.where(col < L, col, 2 * L - col).astype(F32)
            decay = jnp.exp(-(pos / (L - 1)) * dl_ref[...])
            slab = jnp.where(col != L, kt[:, j * LANES:(j + 1) * LANES] * decay, 0.0)
            o_ref[n, :, j * SUBLANES:(j + 1) * SUBLANES, :] = slab.reshape(N_CGROUPS, SUBLANES, LANES)

        @pl.when(i == 0)
        def _():
            back = lax.dot_general(w0_ref[...], h[0:LANES], NT_DIMS, preferred_element_type=F32)
            add = jnp.where(lane == 0, back, 0.0).reshape(N_CGROUPS, SUBLANES, LANES)
            o_ref[n, :, 0:SUBLANES, :] = o_ref[n, :, 0:SUBLANES, :] + add


def _filters(L, w1, b1, f1, w2, b2, f2, w3, tr):
    pad_h = LANES - FILTER_HIDDEN
    z = _filter_features(L)
    w1p = jnp.pad(w1, ((0, LANES - FILTER_EMB), (0, pad_h)))
    w2p = jnp.pad(w2, ((0, pad_h), (0, pad_h)))
    w3t = jnp.pad(w3, ((0, pad_h), (0, 0))).T
    vec = lambda a: jnp.pad(a.reshape(1, -1), ((0, 0), (0, pad_h)))
    deltas = np.abs(np.linspace(math.log(DECAY_TARGET) / SLOW_DECAY_PCT, math.log(DECAY_TARGET) / FAST_DECAY_PCT,
                                D_HYENA)).astype(np.float32)
    deltas = jnp.asarray(np.broadcast_to(deltas[:, None], (D_HYENA, LANES)))
    nt = 2 * L // tr
    half = nt // 2
    const = lambda shape: pl.BlockSpec(shape, lambda i: (0, 0))
    w3spec = lambda n: pl.BlockSpec((D_HYENA, LANES), lambda i: (2 * n + jnp.where(i < half, 0, 1), 0))
    w3back = lambda n: pl.BlockSpec((D_HYENA, LANES), lambda i: (2 * n + 1, 0))
    rows_ct = tr // LANES * SUBLANES
    return pl.pallas_call(
        functools.partial(_filt_kernel, tr=tr, L=L),
        grid=(nt,),
        in_specs=[pl.BlockSpec((tr, LANES), lambda i: (i, 0)),
                  const((LANES, LANES)), const((1, LANES)), const((1, LANES)),
                  const((LANES, LANES)), const((1, LANES)), const((1, LANES)),
                  w3spec(0), w3spec(1), w3back(0), w3back(1), const((D_HYENA, LANES))],
        out_specs=pl.BlockSpec((2, N_CGROUPS, rows_ct, LANES), lambda i: (0, 0, i, 0)),
        out_shape=jax.ShapeDtypeStruct((2, N_CGROUPS, 2 * L // LANES * SUBLANES, LANES), F32),
        compiler_params=_params("arbitrary"),
        name="hyena_filters",
    )(z, w1p, vec(b1), vec(f1), w2p, vec(b2), vec(f2), w3t, w3t, w3t, w3t, deltas)


def _dft_left(n_out, n_in, n, sign):
    a = np.arange(n_out)[:, None]
    b = np.arange(n_in)[None, :]
    ang = 2.0 * np.pi * ((a * b) % n) / n
    c, s = np.cos(ang), sign * np.sin(ang)
    return np.block([[c, -s], [s, c]])


def _dft_right(n, sign):
    return _dft_left(n, n, n, sign).T


def _hyena_kernel(v_ref, x1_ref, x2_ref, gate_ref, kt_ref, sw_ref, sb_ref, d_ref, f1_ref, f1k_ref, f2f_ref, f2i_ref,
                  f3_ref, tc_ref, ts_ref, o_ref, kr_ref, ki_ref, in_ref, out_ref, *, n1):
    n1h = n1 // 2
    n2 = LANES
    nch = SUBLANES
    tc, ts = tc_ref[...], ts_ref[...]
    ts_neg = -ts

    def channel_rows(ref2d, c, n):
        return ref2d[pl.ds(c, n, stride=SUBLANES), :]

    def lanes(mats):
        return jnp.concatenate(mats, axis=1)

    def cmul(ar, ai, br, bi):
        return lanes([ar * br - ai * bi, ar * bi + ai * br])

    def channel_parts(a, c, rows):
        return a[:rows, c * n2:(c + 1) * n2], a[rows:, c * n2:(c + 1) * n2]

    def level2(lhs, mat_ref):
        x = _dot(lhs, mat_ref[...])
        return x[:, :n2], x[:, n2:]

    @pl.when(pl.program_id(1) == 0)
    def _():
        for n in range(2):
            kmat = lanes([channel_rows(kt_ref.at[n, 0], c, n1) for c in range(nch)]).astype(BF16)
            a = _dot(f1k_ref[...], kmat).astype(BF16)
            for c in range(nch):
                ar, ai = channel_parts(a, c, n1)
                xr, xi = level2(cmul(ar, ai, tc, ts_neg), f2f_ref)
                kr_ref[n, c] = (xr * (1.0 / (n1 * n2))).astype(BF16)
                ki_ref[n, c] = (xi * (1.0 / (n1 * n2))).astype(BF16)

    lane = lax.broadcasted_iota(jnp.int32, (n1h, n2), 1)
    row = lax.broadcasted_iota(jnp.int32, (n1h, n2), 0)
    parts = (v_ref, x1_ref, x2_ref, gate_ref)
    for j, ref in enumerate(parts):
        for b in range(2):
            in_ref[j, b] = ref[b, 0].astype(F32)

    def short_conv(part, b, c):
        m = channel_rows(in_ref.at[part, b], c, n1h)
        left = pltpu.roll(m, 1, 1)
        before = jnp.where(lane == 0, jnp.where(row == 0, 0.0, pltpu.roll(left, 1, 0)), left)
        right = pltpu.roll(m, n2 - 1, 1)
        after = jnp.where(lane == n2 - 1, jnp.where(row == n1h - 1, 0.0, pltpu.roll(right, n1h - 1, 0)), right)
        tap = lambda k: sw_ref[k, part, c:c + 1, :]
        return before * tap(0) + m * tap(1) + after * tap(2) + sb_ref[part, c:c + 1, :]

    z = [[short_conv(0, b, c) for c in range(nch)] for b in range(2)]
    for n in range(2):
        stacked = jnp.concatenate([lanes(z[0]), lanes(z[1])], axis=0).astype(BF16)
        a = _dot(f1_ref[...], stacked).astype(BF16)
        brs, bis = [], []
        for c in range(nch):
            ar, ai = channel_parts(a, c, n1)
            xr, xi = level2(cmul(ar, ai, tc, ts_neg), f2f_ref)
            br, bi = level2(cmul(xr.astype(BF16), xi.astype(BF16), kr_ref[n, c], ki_ref[n, c]), f2i_ref)
            back = cmul(br.astype(BF16), bi.astype(BF16), tc, ts)
            brs.append(back[:, :n2])
            bis.append(back[:, n2:])
        y = _dot(f3_ref[...], jnp.concatenate([lanes(brs), lanes(bis)], axis=0))
        for b in range(2):
            for c in range(nch):
                conv = y[b * n1h:(b + 1) * n1h, c * n2:(c + 1) * n2] + z[b][c] * d_ref[n, c:c + 1, :]
                nxt = short_conv(n + 1, b, c) * conv
                if n == 0:
                    z[b][c] = nxt
                else:
                    gate = channel_rows(in_ref.at[3, b], c, n1h)
                    out_ref.at[b][pl.ds(c, n1h, stride=SUBLANES), :] = nxt * gate
    for b in range(2):
        o_ref[b, 0] = out_ref[b].astype(o_ref.dtype)


def _hyena(u_ct, kt_ct, short_w, short_b, hyena_d, n1):
    B, _, R, _ = u_ct.shape
    n1h, n2 = n1 // 2, LANES
    bf = lambda m: jnp.asarray(m, dtype=BF16)
    f1 = bf(_dft_left(n1, n1h, n1, -1))
    f1k = bf(_dft_left(n1, n1, n1, -1)[:, :n1])
    f2f, f2i = bf(_dft_right(n2, -1)), bf(_dft_right(n2, +1))
    f3 = bf(_dft_left(n1h, n1, n1, +1))
    ang = 2.0 * np.pi * ((np.arange(n1)[:, None] * np.arange(n2)[None, :]) % (n1 * n2)) / (n1 * n2)
    tc, ts = bf(np.cos(ang)), bf(np.sin(ang))
    wide = lambda t: jnp.broadcast_to(t[..., None], t.shape + (LANES,))
    sw = wide(short_w).reshape(3, 3, N_CGROUPS, SUBLANES, LANES)
    sb = wide(short_b).reshape(3, N_CGROUPS, SUBLANES, LANES)
    dd = wide(hyena_d).reshape(2, N_CGROUPS, SUBLANES, LANES)

    part = lambda j: pl.BlockSpec((2, 1, R, LANES), lambda g, p: (p, g + j * N_CGROUPS, 0, 0))
    const = lambda a: pl.BlockSpec(a.shape, lambda g, p: (0,) * a.ndim)
    mats = (f1, f1k, f2f, f2i, f3, tc, ts)
    return pl.pallas_call(
        functools.partial(_hyena_kernel, n1=n1),
        grid=(N_CGROUPS, B // 2),
        in_specs=[part(0), part(1), part(2), part(3),
                  pl.BlockSpec((2, 1, 2 * R, LANES), lambda g, p: (0, g, 0, 0)),
                  pl.BlockSpec((3, 3, None, SUBLANES, LANES), lambda g, p: (0, 0, g, 0, 0)),
                  pl.BlockSpec((3, None, SUBLANES, LANES), lambda g, p: (0, g, 0, 0)),
                  pl.BlockSpec((2, None, SUBLANES, LANES), lambda g, p: (0, g, 0, 0))]
                 + [const(m) for m in mats],
        out_specs=pl.BlockSpec((2, 1, R, LANES), lambda g, p: (p, g, 0, 0)),
        out_shape=jax.ShapeDtypeStruct((B, N_CGROUPS, R, LANES), BF16),
        scratch_shapes=[pltpu.VMEM((2, SUBLANES, n1, n2), BF16), pltpu.VMEM((2, SUBLANES, n1, n2), BF16),
                        pltpu.VMEM((4, 2, R, LANES), F32), pltpu.VMEM((2, R, LANES), F32)],
        compiler_params=_params("parallel", "arbitrary"),
        name="hyena",
    )(u_ct, u_ct, u_ct, u_ct, kt_ct, sw, sb, dd, *mats)


def _out_kernel(o1_ref, o4_ref, o16_ref, l1_ref, l4_ref, l16_ref, gates_ref, hg_ref, x_ref, mod_ref,
                fg_ref, e_ref, wpa_ref, wph_ref, wo_ref, out_ref, slab_ref):
    tm = x_ref.shape[1]

    def token_order(ref, r):
        if r == 1:
            return ref[0].astype(F32)
        nslab = ref.shape[-1] // LANES
        for rho in range(r):
            for t in range(nslab):
                slab_ref.at[t][pl.ds(rho, tm // r, stride=r), :] = ref[rho, :, t * LANES:(t + 1) * LANES].astype(F32)
        return jnp.concatenate([slab_ref[t] for t in range(nslab)], axis=1)

    l1, l4, l16 = (token_order(ref, r) for ref, r in zip((l1_ref, l4_ref, l16_ref), DILATIONS))
    mx = jnp.maximum(jnp.maximum(l1, l4), l16)
    e1, e4, e16 = jnp.exp(l1 - mx), jnp.exp(l4 - mx), jnp.exp(l16 - mx)
    tot = e1 + e4 + e16
    heads = lambda a: _dot((a / tot).astype(BF16), e_ref[...])
    oa = heads(e1) * token_order(o1_ref, 1)
    oa = oa + heads(e4) * token_order(o4_ref, 4)
    oa = oa + heads(e16) * token_order(o16_ref, 16)
    gate = lambda c0, width: gates_ref[0, :, c0:c0 + width].astype(F32)
    ya = _dot((oa * gate(G_GA, D_ATTN)).astype(BF16), wpa_ref[...])
    hg = hg_ref[0].astype(F32)
    yh = jnp.concatenate(
        [lax.dot_general(hg[:, j * SUBLANES:(j + 1) * SUBLANES, :].reshape(D_HYENA, LANES).astype(BF16),
                         wph_ref[...], TN_DIMS, preferred_element_type=F32) for j in range(tm // LANES)], axis=0)
    mixed = gate(G_MA, D_MODEL) * ya + gate(G_MH, D_MODEL) * yh
    y = x_ref[0] + mod_ref[0, 2:3, :] * _dot(mixed.astype(BF16), wo_ref[...])
    ms = jnp.mean(y * y, axis=-1, keepdims=True)
    out_ref[0] = y * lax.rsqrt(ms + EPS) * fg_ref[...]


def _out_proj(o_pat, lse_pat, gates, hg_ct, x, mod, final_g, wpa, wph, wo, tm):
    B, L, _ = x.shape
    row = lambda width: pl.BlockSpec((1, tm, width), lambda b, i: (b, i, 0))
    strided = lambda r, width: pl.BlockSpec((None, r, tm // r, width), lambda b, i: (b, 0, i, 0))
    const = lambda shape: pl.BlockSpec(shape, lambda b, i: (0,) * len(shape))
    expand = np.zeros((LANES, D_ATTN), np.float32)
    expand[np.arange(D_ATTN) // HEAD_DIM, np.arange(D_ATTN)] = 1.0
    return pl.pallas_call(
        _out_kernel,
        grid=(B, L // tm),
        in_specs=[strided(r, D_ATTN) for r in DILATIONS] + [strided(r, LANES) for r in DILATIONS]
                 + [row(D_GATES),
                    pl.BlockSpec((1, N_CGROUPS, tm // LANES * SUBLANES, LANES), lambda b, i: (b, 0, i, 0)),
                    row(D_MODEL),
                    pl.BlockSpec((1, 3, D_MODEL), lambda b, i: (b, 0, 0)),
                    const((1, D_MODEL)), const((LANES, D_ATTN)),
                    const((D_ATTN, D_MODEL)), const((D_HYENA, D_MODEL)), const((D_MODEL, D_MODEL))],
        out_specs=row(D_MODEL),
        out_shape=jax.ShapeDtypeStruct((B, L, D_MODEL), F32),
        scratch_shapes=[pltpu.VMEM((D_ATTN // LANES, tm, LANES), F32)],
        compiler_params=_params("parallel", "parallel"),
        name="out_proj",
    )(*o_pat, *lse_pat, gates, hg_ct, x, mod, final_g.reshape(1, -1), jnp.asarray(expand, dtype=BF16),
      wpa, wph, wo)


def _trunk(x, mod, wts):
    B, L, _ = x.shape
    assert L % 512 == 0 and B % 2 == 0
    tm = 512
    *qkv, gates, u_ct = _inproj(x, mod, wts["norm_g"], wts["w_rows"], wts["w_hy_t"], tm)
    pats = [_band_attention(t, wts["rel_bias"]) for t in qkv]
    kt_ct = _filters(L, *wts["filt"], tr=512)
    hg_ct = _hyena(u_ct, kt_ct, wts["short_w"], wts["short_b"], wts["hyena_d"], 2 * L // LANES)
    return _out_proj([p[0] for p in pats], [p[1] for p in pats], gates, hg_ct, x, mod, wts["final_g"],
                     wts["w_proj_attn"], wts["w_proj_hyena"], wts["w_out"], tm)


def kernel(x_prompt, x_sample, c_prompt, c_sample, w_ada, b_ada, norm_g, w_in, short_w, short_b, filt_w1, filt_b1,
           filt_freq1, filt_w2, filt_b2, filt_freq2, filt_w3, hyena_d, w_proj_attn, w_proj_hyena, w_out, rel_bias,
           final_g):
    assert w_ada.shape[0] == 1, "single layer"
    bp, bs = c_prompt.shape[0], c_sample.shape[0]
    c_all = jnp.concatenate([c_prompt, c_sample], axis=0)
    rows = -(-(bp + bs) // SUBLANES) * SUBLANES
    c_all = jnp.pad(c_all, ((0, rows - bp - bs), (0, 0)))
    mod = _ada(c_all, w_ada[0], b_ada[0]).reshape(rows, 3, D_MODEL)
    w = w_in[0].astype(BF16)
    wts = dict(norm_g=norm_g[0],
               w_rows=jnp.concatenate([w[:, :C_U], w[:, C_MA:]], axis=1),
               w_hy_t=w[:, C_U:C_MA].T,
               short_w=short_w[0], short_b=short_b[0],
               filt=(filt_w1[0], filt_b1[0], filt_freq1[0], filt_w2[0], filt_b2[0], filt_freq2[0], filt_w3[0]),
               hyena_d=hyena_d[0], w_proj_attn=w_proj_attn[0].astype(BF16),
               w_proj_hyena=w_proj_hyena[0].astype(BF16), w_out=w_out[0].astype(BF16), rel_bias=rel_bias,
               final_g=final_g)
    y_prompt = _trunk(x_prompt, mod[:bp], wts)
    y_sample = _trunk(x_sample, mod[bp:bp + bs], wts)
    return (y_prompt, y_sample)
```

```python
import functools
import math

import numpy as np
import jax
import jax.numpy as jnp
from jax import lax
from jax.experimental import pallas as pl
from jax.experimental.pallas import tpu as pltpu

F32 = jnp.float32
BF16 = jnp.bfloat16

D_MODEL = 1024
N_HEADS = 12
HEAD_DIM = 64
D_ATTN = N_HEADS * HEAD_DIM
N_PAIRS = N_HEADS // 2
DILATIONS = (1, 4, 16)
HALF = 64
N_BUCKETS = 32
MAX_DISTANCE = 1024
D_HYENA = 768
FILTER_EMB = 33
FILTER_HIDDEN = 64
FAST_DECAY_PCT = 0.3
SLOW_DECAY_PCT = 1.5
DECAY_TARGET = 0.01
EPS = 1e-6
NEG_INF = -1e30
LANES = 128
SUBLANES = 8
BF16_ROWS = 16
VMEM_LIMIT = 56 * 1024 * 1024
N_CGROUPS = D_HYENA // SUBLANES

C_Q, C_K, C_V, C_GA = 0, 768, 1536, 2304
C_U = 3072
C_GH = C_U + 3 * D_HYENA
C_MA = C_GH + D_HYENA
C_MH = C_MA + D_MODEL
D_PROJ = C_MH + D_MODEL
R_Q, R_K, R_V, R_GA, R_MA, R_MH = 0, 768, 1536, 2304, 3072, 4096
D_ROWMAJOR = 5120
G_GA, G_MA, G_MH = 0, 768, 1792
D_GATES = 2816
INPROJ_CHUNK = 256

NT_DIMS = (((1,), (1,)), ((), ()))
TN_DIMS = (((0,), (0,)), ((), ()))


def _params(*sem):
    return pltpu.CompilerParams(dimension_semantics=sem, vmem_limit_bytes=VMEM_LIMIT)


def _sigmoid(x):
    return 1.0 / (1.0 + jnp.exp(-x))


def _dot(a, b):
    return jnp.dot(a, b, preferred_element_type=F32)


def _to_channel_tiled(ref, lead, value):
    groups = value.shape[0] // SUBLANES
    band = lambda j: value[:, j * LANES:(j + 1) * LANES].reshape(groups, SUBLANES, LANES)
    for j in range(0, value.shape[1] // LANES, 2):
        rows = slice(j * SUBLANES, (j + 2) * SUBLANES)
        ref[lead + (rows, slice(None))] = jnp.concatenate([band(j), band(j + 1)], axis=1).astype(ref.dtype)


def _ada_kernel(c_ref, w_ref, b_ref, o_ref):
    c = c_ref[...]
    s = c * _sigmoid(c)
    o_ref[...] = jnp.dot(s, w_ref[...], preferred_element_type=F32,
                         precision=lax.Precision.HIGHEST) + b_ref[...]


def _ada(c_pad, w_ada, b_ada):
    rows = c_pad.shape[0]
    return pl.pallas_call(
        _ada_kernel,
        grid=(3,),
        in_specs=[pl.BlockSpec((rows, D_MODEL), lambda j: (0, 0)),
                  pl.BlockSpec((D_MODEL, D_MODEL), lambda j: (0, j)),
                  pl.BlockSpec((1, D_MODEL), lambda j: (0, j))],
        out_specs=pl.BlockSpec((rows, D_MODEL), lambda j: (0, j)),
        out_shape=jax.ShapeDtypeStruct((rows, 3 * D_MODEL), F32),
        compiler_params=_params("arbitrary"),
        name="ada",
    )(c_pad, w_ada, b_ada.reshape(1, -1))


def _inproj_kernel(x_ref, mod_ref, g_ref, w_ref, wt_ref, qkv1_ref, qkv4_ref, qkv16_ref, gates_ref, ct_ref, slab_ref):
    x = x_ref[0]
    tm = x.shape[0]
    ms = jnp.mean(x * x, axis=-1, keepdims=True)
    y = x * lax.rsqrt(ms + EPS) * g_ref[...]
    h = (y * (1.0 + mod_ref[0, 1:2, :]) + mod_ref[0, 0:1, :]).astype(BF16)
    silu = lambda g: g * _sigmoid(g)

    def store_strided(which, scale, c0):
        def store(val):
            val = val * scale if scale != 1.0 else val
            width = val.shape[1]
            dst = slice(c0, c0 + width)
            qkv1_ref[0, which, 0, :, dst] = val.astype(BF16)
            slabs = range(c0 // LANES, (c0 + width) // LANES)
            for t in slabs:
                slab_ref[t] = val[:, t * LANES - c0:(t + 1) * LANES - c0]
            for r, o_ref in zip(DILATIONS[1:], (qkv4_ref, qkv16_ref)):
                for rho in range(r):
                    rows = [slab_ref.at[t][pl.ds(rho, tm // r, stride=r), :] for t in slabs]
                    o_ref[0, which, rho, :, dst] = jnp.concatenate(rows, axis=1).astype(BF16)
        return store

    def gate_cols(c0, fn):
        def store(val):
            gates_ref[0, :, c0:c0 + val.shape[1]] = fn(val).astype(BF16)
        return store

    def channel_tiled(g0, fn):
        return lambda val: _to_channel_tiled(ct_ref, (0, slice(g0, g0 + val.shape[0] // SUBLANES)), fn(val))

    rows = lambda c0: lambda: _dot(h, w_ref[:, c0:c0 + INPROJ_CHUNK])
    cols = lambda r0: lambda: lax.dot_general(wt_ref[r0:r0 + INPROJ_CHUNK, :], h, NT_DIMS,
                                              preferred_element_type=F32)
    ident = lambda v: v
    chunks = lambda width: range(0, width, INPROJ_CHUNK)
    stages = []
    for which, (r0, scale) in enumerate(((R_Q, 1.0 / math.sqrt(HEAD_DIM)), (R_K, 1.0), (R_V, 1.0))):
        stages += [(rows(r0 + c0), store_strided(which, scale, c0)) for c0 in chunks(D_ATTN)]
    stages += [(rows(R_GA + c0), gate_cols(G_GA + c0, silu)) for c0 in chunks(D_ATTN)]
    stages += [(rows(R_MA + c0), gate_cols(G_MA + c0, _sigmoid)) for c0 in chunks(2 * D_MODEL)]
    stages += [(cols(r0), channel_tiled(r0 // SUBLANES, ident if r0 < 3 * D_HYENA else silu))
               for r0 in chunks(4 * D_HYENA)]
    pending = stages[0][0]()
    for j, (_, finish) in enumerate(stages):
        nxt = stages[j + 1][0]() if j + 1 < len(stages) else None
        finish(pending)
        pending = nxt


def _inproj(x, mod, norm_g, w_rows, w_hy_t, tm):
    B, L, _ = x.shape
    rows_ct = tm // LANES * SUBLANES
    strided = lambda r: pl.BlockSpec((1, 3, r, tm // r, D_ATTN), lambda b, i: (b, 0, 0, i, 0))
    strided_shape = lambda r: jax.ShapeDtypeStruct((B, 3, r, L // r, D_ATTN), BF16)
    resident = lambda shape: pl.BlockSpec(shape, lambda b, i: (0, 0), pipeline_mode=pl.Buffered(1))
    return pl.pallas_call(
        _inproj_kernel,
        grid=(B, L // tm),
        in_specs=[pl.BlockSpec((1, tm, D_MODEL), lambda b, i: (b, i, 0)),
                  pl.BlockSpec((1, 3, D_MODEL), lambda b, i: (b, 0, 0)),
                  pl.BlockSpec((1, D_MODEL), lambda b, i: (0, 0)),
                  resident((D_MODEL, D_ROWMAJOR)), resident((4 * D_HYENA, D_MODEL))],
        out_specs=[strided(r) for r in DILATIONS]
                  + [pl.BlockSpec((1, tm, D_GATES), lambda b, i: (b, i, 0)),
                     pl.BlockSpec((1, 4 * N_CGROUPS, rows_ct, LANES), lambda b, i: (b, 0, i, 0))],
        out_shape=[strided_shape(r) for r in DILATIONS]
                  + [jax.ShapeDtypeStruct((B, L, D_GATES), BF16),
                     jax.ShapeDtypeStruct((B, 4 * N_CGROUPS, L // LANES * SUBLANES, LANES), BF16)],
        scratch_shapes=[pltpu.VMEM((D_ATTN // LANES, tm, LANES), F32)],
        compiler_params=_params("parallel", "parallel"),
        name="inproj",
    )(x, mod, norm_g.reshape(1, -1), w_rows, w_hy_t)


def _t5_bucket(rel):
    nb = N_BUCKETS // 2
    max_exact = nb // 2
    n = np.abs(rel)
    large = max_exact + (np.log(np.maximum(n, 1) / max_exact) / math.log(MAX_DISTANCE / max_exact)
                         * (nb - max_exact)).astype(np.int32)
    large = np.minimum(large, nb - 1)
    return ((rel > 0).astype(np.int32) * nb + np.where(n < max_exact, n, large)).astype(np.int32)


def _bias_table(rel_bias, r, qb):
    a = np.arange(qb)[:, None]
    b = np.arange(qb + 2 * HALF)[None, :]
    rel = b - HALF - a
    valid = np.abs(rel) <= HALF
    onehot = (jnp.asarray(_t5_bucket(rel * r))[..., None] == jnp.arange(N_BUCKETS)).astype(F32)
    bias = jnp.dot(onehot, rel_bias.astype(F32), precision=lax.Precision.HIGHEST)
    bias = jnp.where(valid[..., None], bias, NEG_INF).transpose(2, 0, 1)
    bias = bias.reshape(N_PAIRS, 2 * qb, qb + 2 * HALF)
    left = b < HALF
    right = b >= qb + HALF
    return jnp.stack([bias, jnp.where(left, NEG_INF, bias), jnp.where(right, NEG_INF, bias),
                      jnp.where(left | right, NEG_INF, bias)])


STAT_L_LANE = 16


def _attn_kernel(q_ref, kp_ref, kc_ref, kn_ref, vp_ref, vc_ref, vn_ref, bias_ref, o_ref, stat_ref, *, mb, qb, lr):
    i = pl.program_id(1)
    nsub = mb // qb
    lane = lax.broadcasted_iota(jnp.int32, (qb, LANES), 1)
    lo = lane < HEAD_DIM
    first = (i == 0).astype(jnp.int32)
    last = (i == lr // mb - 1).astype(jnp.int32)

    def window(p_ref, c_ref, n_ref, s, cols):
        parts = []
        if s == 0:
            parts.append(p_ref[:, cols])
        lo_row = max(s * qb - HALF, 0)
        hi_row = min((s + 1) * qb + HALF, mb)
        parts.append(c_ref[lo_row:hi_row, cols])
        if s == nsub - 1:
            parts.append(n_ref[:, cols])
        return parts[0] if len(parts) == 1 else jnp.concatenate(parts, axis=0)

    for s in range(nsub):
        variant = (first if s == 0 else 0) + 2 * (last if s == nsub - 1 else 0)
        stats = jnp.zeros((qb, LANES), F32)
        for p in range(N_PAIRS):
            cols = slice(p * LANES, (p + 1) * LANES)
            qp = q_ref[s * qb:(s + 1) * qb, cols]
            zero = jnp.zeros_like(qp)
            q2 = jnp.concatenate([jnp.where(lo, qp, zero), jnp.where(lo, zero, qp)], axis=0)
            kwin = window(kp_ref, kc_ref, kn_ref, s, cols)
            vwin = window(vp_ref, vc_ref, vn_ref, s, cols)
            logits = lax.dot_general(q2, kwin, NT_DIMS, preferred_element_type=F32) + bias_ref[variant, p]
            m = jnp.max(logits, axis=-1, keepdims=True)
            e = jnp.exp(logits - m)
            l = jnp.sum(e, axis=-1, keepdims=True)
            o = _dot(e.astype(BF16), vwin)
            o_ref[s * qb:(s + 1) * qb, cols] = jnp.where(lo, o[:qb], o[qb:]).astype(BF16)
            for h, (mh, lh) in enumerate(((m[:qb], l[:qb]), (m[qb:], l[qb:]))):
                stats = jnp.where(lane == 2 * p + h, mh, stats)
                stats = jnp.where(lane == STAT_L_LANE + 2 * p + h, lh, stats)
        stat_ref[s * qb:(s + 1) * qb, :] = stats


def _band_attention(qkv, rel_bias):
    B, _, r, lr, _ = qkv.shape
    qb = 128
    mb = min(512, lr)
    hb = mb // HALF
    nhalo = lr // HALF
    cur_rows = lambda i: i
    prev_rows = lambda i: jnp.maximum(i * hb - 1, 0)
    next_rows = lambda i: jnp.minimum((i + 1) * hb, nhalo - 1)
    part = lambda which, rows, idx: pl.BlockSpec((None, None, None, rows, D_ATTN),
                                                 lambda b, i, rho: (b, which, rho, idx(i), 0))
    out = lambda width: pl.BlockSpec((None, None, mb, width), lambda b, i, rho: (b, rho, i, 0))
    bias = _bias_table(rel_bias, r, qb)
    return pl.pallas_call(
        functools.partial(_attn_kernel, mb=mb, qb=qb, lr=lr),
        grid=(B, lr // mb, r),
        in_specs=[part(0, mb, cur_rows),
                  part(1, HALF, prev_rows), part(1, mb, cur_rows), part(1, HALF, next_rows),
                  part(2, HALF, prev_rows), part(2, mb, cur_rows), part(2, HALF, next_rows),
                  pl.BlockSpec(bias.shape, lambda b, i, rho: (0, 0, 0, 0), pipeline_mode=pl.Buffered(1))],
        out_specs=[out(D_ATTN), out(LANES)],
        out_shape=[jax.ShapeDtypeStruct((B, r, lr, D_ATTN), BF16),
                   jax.ShapeDtypeStruct((B, r, lr, LANES), F32)],
        compiler_params=_params("parallel", "parallel", "parallel"),
        name=f"attn_r{r}",
    )(*([qkv] * 7), bias)


def _filter_features(L):
    j = jnp.arange(2 * L, dtype=jnp.int32)
    pos = jnp.where(j < L, j, 2 * L - j).astype(F32)
    t = (pos / (L - 1))[:, None]
    bands = (FILTER_EMB - 1) // 2
    w = (2.0 * math.pi / L) * pos[:, None]
    f = jnp.linspace(1e-4, bands - 1, bands, dtype=F32)[None, :]
    z = jnp.concatenate([t, jnp.cos(f * w), -jnp.sin(f * w)], axis=-1)
    return jnp.pad(z, ((0, 0), (0, LANES - FILTER_EMB)))


def _filt_kernel(z_ref, w1_ref, w1s_ref, b1_ref, f1_ref, w2_ref, b2_ref, f2_ref, w3a_lo_ref, w3a_hi_ref, w3b_lo_ref,
                 w3b_hi_ref, w3a0_ref, w3b0_ref, dl_ref, o_ref, *, tr, L):
    i = pl.program_id(0)
    half = tr // 2
    pre = _dot(z_ref[0:half, :], w1_ref[...]) + _dot(z_ref[half:tr, :], w1s_ref[...])
    h = jnp.sin(f1_ref[...] * (pre + b1_ref[...]))
    h = jnp.sin(f2_ref[...] * (_dot(h, w2_ref[...]) + b2_ref[...]))
    lane = lax.broadcasted_iota(jnp.int32, (1, LANES), 1)
    for n, (lo_ref, hi_ref, w0_ref) in enumerate(((w3a_lo_ref, w3a_hi_ref, w3a0_ref),
                                                  (w3b_lo_ref, w3b_hi_ref, w3b0_ref))):
        nt = lambda w_ref: lax.dot_general(w_ref[...], h, NT_DIMS, preferred_element_type=F32)
        kt = jnp.concatenate([nt(lo_ref), nt(hi_ref)], axis=1)
        for j in range(tr // LANES):
            col = i * tr + j * LANES + lane
            pos = jnp.where(col < L, col, 2 * L - col).astype(F32)
            decay = jnp.exp(-(pos / (L - 1)) * dl_ref[...])
            slab = jnp.where(col != L, kt[:, j * LANES:(j + 1) * LANES] * decay, 0.0)
            o_ref[n, :, j * SUBLANES:(j + 1) * SUBLANES, :] = slab.reshape(N_CGROUPS, SUBLANES, LANES)

        @pl.when(i == 0)
        def _():
            back = lax.dot_general(w0_ref[...], h[0:LANES], NT_DIMS, preferred_element_type=F32)
            add = jnp.where(lane == 0, back, 0.0).reshape(N_CGROUPS, SUBLANES, LANES)
            o_ref[n, :, 0:SUBLANES, :] = o_ref[n, :, 0:SUBLANES, :] + add


def _filters(L, w1, b1, f1, w2, b2, f2, w3, tr):
    hid = FILTER_HIDDEN
    z = _filter_features(L)
    w1_lo = jnp.pad(w1, ((0, LANES - FILTER_EMB), (0, LANES - hid)))
    w1_hi = jnp.pad(w1, ((0, LANES - FILTER_EMB), (LANES - hid, 0)))
    w2_bd = jnp.pad(w2, ((0, LANES - hid), (0, LANES - hid))) + jnp.pad(w2, ((LANES - hid, 0), (LANES - hid, 0)))
    w3t_lo = jnp.pad(w3, ((0, LANES - hid), (0, 0))).T
    w3t_hi = jnp.pad(w3, ((LANES - hid, 0), (0, 0))).T
    vec = lambda a: jnp.tile(a.reshape(1, -1), (1, LANES // hid))
    deltas = np.abs(np.linspace(math.log(DECAY_TARGET) / SLOW_DECAY_PCT, math.log(DECAY_TARGET) / FAST_DECAY_PCT,
                                D_HYENA)).astype(np.float32)
    deltas = jnp.asarray(np.broadcast_to(deltas[:, None], (D_HYENA, LANES)))
    nt = 2 * L // tr
    half = nt // 2
    const = lambda shape: pl.BlockSpec(shape, lambda i: (0, 0))
    w3spec = lambda n: pl.BlockSpec((D_HYENA, LANES), lambda i: (2 * n + jnp.where(i < half, 0, 1), 0))
    w3back = lambda n: pl.BlockSpec((D_HYENA, LANES), lambda i: (2 * n + 1, 0))
    rows_ct = tr // LANES * SUBLANES
    return pl.pallas_call(
        functools.partial(_filt_kernel, tr=tr, L=L),
        grid=(nt,),
        in_specs=[pl.BlockSpec((tr, LANES), lambda i: (i, 0)),
                  const((LANES, LANES)), const((LANES, LANES)), const((1, LANES)), const((1, LANES)),
                  const((LANES, LANES)), const((1, LANES)), const((1, LANES)),
                  w3spec(0), w3spec(0), w3spec(1), w3spec(1), w3back(0), w3back(1), const((D_HYENA, LANES))],
        out_specs=pl.BlockSpec((2, N_CGROUPS, rows_ct, LANES), lambda i: (0, 0, i, 0)),
        out_shape=jax.ShapeDtypeStruct((2, N_CGROUPS, 2 * L // LANES * SUBLANES, LANES), F32),
        compiler_params=_params("arbitrary"),
        name="hyena_filters",
    )(z, w1_lo, w1_hi, vec(b1), vec(f1), w2_bd, vec(b2), vec(f2), w3t_lo, w3t_hi, w3t_lo, w3t_hi, w3t_lo, w3t_lo,
      deltas)


def _dft_left(n_out, n_in, n, sign):
    a = np.arange(n_out)[:, None]
    b = np.arange(n_in)[None, :]
    ang = 2.0 * np.pi * ((a * b) % n) / n
    c, s = np.cos(ang), sign * np.sin(ang)
    return np.block([[c, -s], [s, c]])


def _dft_right(n, sign):
    return _dft_left(n, n, n, sign).T


def _hyena_kernel(v_ref, x1_ref, x2_ref, gate_ref, kt_ref, sw_ref, sb_ref, d_ref, f1_ref, f1k_ref, f2f_ref, f2i_ref,
                  f3_ref, tc_ref, ts_ref, o_ref, kr_ref, ki_ref, in_ref, out_ref, *, n1):
    n1h = n1 // 2
    n2 = LANES
    nch = SUBLANES
    tc, ts = tc_ref[...], ts_ref[...]
    ts_neg = -ts

    def channel_rows(ref2d, c, n):
        return ref2d[pl.ds(c, n, stride=SUBLANES), :]

    def lanes(mats):
        return jnp.concatenate(mats, axis=1)

    def cmul(ar, ai, br, bi):
        return lanes([ar * br - ai * bi, ar * bi + ai * br])

    def channel_parts(a, c, rows):
        return a[:rows, c * n2:(c + 1) * n2], a[rows:, c * n2:(c + 1) * n2]

    def level2(lhs, mat_ref):
        x = _dot(lhs, mat_ref[...])
        return x[:, :n2], x[:, n2:]

    @pl.when(pl.program_id(1) == 0)
    def _():
        for n in range(2):
            kmat = lanes([channel_rows(kt_ref.at[n, 0], c, n1) for c in range(nch)]).astype(BF16)
            a = _dot(f1k_ref[...], kmat).astype(BF16)
            for c in range(nch):
                ar, ai = channel_parts(a, c, n1)
                xr, xi = level2(cmul(ar, ai, tc, ts_neg), f2f_ref)
                kr_ref[n, c] = (xr * (1.0 / (n1 * n2))).astype(BF16)
                ki_ref[n, c] = (xi * (1.0 / (n1 * n2))).astype(BF16)

    lane = lax.broadcasted_iota(jnp.int32, (n1h, n2), 1)
    row = lax.broadcasted_iota(jnp.int32, (n1h, n2), 0)
    parts = (v_ref, x1_ref, x2_ref, gate_ref)
    for j, ref in enumerate(parts):
        for b in range(2):
            in_ref[j, b] = ref[b, 0].astype(F32)

    def short_conv(part, b, c):
        m = channel_rows(in_ref.at[part, b], c, n1h)
        left = pltpu.roll(m, 1, 1)
        before = jnp.where(lane == 0, jnp.where(row == 0, 0.0, pltpu.roll(left, 1, 0)), left)
        right = pltpu.roll(m, n2 - 1, 1)
        after = jnp.where(lane == n2 - 1, jnp.where(row == n1h - 1, 0.0, pltpu.roll(right, n1h - 1, 0)), right)
        tap = lambda k: sw_ref[k, part, c:c + 1, :]
        return before * tap(0) + m * tap(1) + after * tap(2) + sb_ref[part, c:c + 1, :]

    z = [[short_conv(0, b, c) for c in range(nch)] for b in range(2)]
    for n in range(2):
        stacked = jnp.concatenate([lanes(z[0]), lanes(z[1])], axis=0).astype(BF16)
        a = _dot(f1_ref[...], stacked).astype(BF16)
        brs, bis = [], []
        for c in range(nch):
            ar, ai = channel_parts(a, c, n1)
            xr, xi = level2(cmul(ar, ai, tc, ts_neg), f2f_ref)
            br, bi = level2(cmul(xr.astype(BF16), xi.astype(BF16), kr_ref[n, c], ki_ref[n, c]), f2i_ref)
            back = cmul(br.astype(BF16), bi.astype(BF16), tc, ts)
            brs.append(back[:, :n2])
            bis.append(back[:, n2:])
        y = _dot(f3_ref[...], jnp.concatenate([lanes(brs), lanes(bis)], axis=0))
        for b in range(2):
            for c in range(nch):
                conv = y[b * n1h:(b + 1) * n1h, c * n2:(c + 1) * n2] + z[b][c] * d_ref[n, c:c + 1, :]
                nxt = short_conv(n + 1, b, c) * conv
                if n == 0:
                    z[b][c] = nxt
                else:
                    gate = channel_rows(in_ref.at[3, b], c, n1h)
                    out_ref.at[b][pl.ds(c, n1h, stride=SUBLANES), :] = nxt * gate
    for b in range(2):
        o_ref[b, 0] = out_ref[b].astype(o_ref.dtype)


def _hyena(u_ct, kt_ct, short_w, short_b, hyena_d, n1):
    B, _, R, _ = u_ct.shape
    n1h, n2 = n1 // 2, LANES
    bf = lambda m: jnp.asarray(m, dtype=BF16)
    f1 = bf(_dft_left(n1, n1h, n1, -1))
    f1k = bf(_dft_left(n1, n1, n1, -1)[:, :n1])
    f2f, f2i = bf(_dft_right(n2, -1)), bf(_dft_right(n2, +1))
    f3 = bf(_dft_left(n1h, n1, n1, +1))
    ang = 2.0 * np.pi * ((np.arange(n1)[:, None] * np.arange(n2)[None, :]) % (n1 * n2)) / (n1 * n2)
    tc, ts = bf(np.cos(ang)), bf(np.sin(ang))
    wide = lambda t: jnp.broadcast_to(t[..., None], t.shape + (LANES,))
    sw = wide(short_w).reshape(3, 3, N_CGROUPS, SUBLANES, LANES)
    sb = wide(short_b).reshape(3, N_CGROUPS, SUBLANES, LANES)
    dd = wide(hyena_d).reshape(2, N_CGROUPS, SUBLANES, LANES)

    part = lambda j: pl.BlockSpec((2, 1, R, LANES), lambda g, p: (p, g + j * N_CGROUPS, 0, 0))
    const = lambda a: pl.BlockSpec(a.shape, lambda g, p: (0,) * a.ndim)
    mats = (f1, f1k, f2f, f2i, f3, tc, ts)
    return pl.pallas_call(
        functools.partial(_hyena_kernel, n1=n1),
        grid=(N_CGROUPS, B // 2),
        in_specs=[part(0), part(1), part(2), part(3),
                  pl.BlockSpec((2, 1, 2 * R, LANES), lambda g, p: (0, g, 0, 0)),
                  pl.BlockSpec((3, 3, None, SUBLANES, LANES), lambda g, p: (0, 0, g, 0, 0)),
                  pl.BlockSpec((3, None, SUBLANES, LANES), lambda g, p: (0, g, 0, 0)),
                  pl.BlockSpec((2, None, SUBLANES, LANES), lambda g, p: (0, g, 0, 0))]
                 + [const(m) for m in mats],
        out_specs=pl.BlockSpec((2, 1, R, LANES), lambda g, p: (p, g, 0, 0)),
        out_shape=jax.ShapeDtypeStruct((B, N_CGROUPS, R, LANES), BF16),
        scratch_shapes=[pltpu.VMEM((2, SUBLANES, n1, n2), BF16), pltpu.VMEM((2, SUBLANES, n1, n2), BF16),
                        pltpu.VMEM((4, 2, R, LANES), F32), pltpu.VMEM((2, R, LANES), F32)],
        compiler_params=_params("parallel", "arbitrary"),
        name="hyena",
    )(u_ct, u_ct, u_ct, u_ct, kt_ct, sw, sb, dd, *mats)


def _out_kernel(o1_ref, o4_ref, o16_ref, l1_ref, l4_ref, l16_ref, gates_ref, hg_ref, x_ref, mod_ref,
                fg_ref, e_ref, wpa_ref, wph_ref, wo_ref, out_ref, slab_ref):
    tm = x_ref.shape[1]

    def token_order(ref, r):
        if r == 1:
            return ref[0].astype(F32)
        nslab = ref.shape[-1] // LANES
        for rho in range(r):
            for t in range(nslab):
                slab_ref.at[t][pl.ds(rho, tm // r, stride=r), :] = ref[rho, :, t * LANES:(t + 1) * LANES].astype(F32)
        return jnp.concatenate([slab_ref[t] for t in range(nslab)], axis=1)

    stats = [token_order(ref, r) for ref, r in zip((l1_ref, l4_ref, l16_ref), DILATIONS)]
    mx = jnp.maximum(jnp.maximum(stats[0], stats[1]), stats[2])
    w = [jnp.exp(st - mx) for st in stats]
    rowsum = [pltpu.roll(st, LANES - STAT_L_LANE, 1) for st in stats]
    tot = w[0] * rowsum[0] + w[1] * rowsum[1] + w[2] * rowsum[2]
    head_lane = lax.broadcasted_iota(jnp.int32, mx.shape, 1) < N_HEADS
    heads = lambda a: _dot(jnp.where(head_lane, a / tot, 0.0).astype(BF16), e_ref[...])
    oa = heads(w[0]) * token_order(o1_ref, 1)
    oa = oa + heads(w[1]) * token_order(o4_ref, 4)
    oa = oa + heads(w[2]) * token_order(o16_ref, 16)
    gate = lambda c0, width: gates_ref[0, :, c0:c0 + width].astype(F32)
    ya = _dot((oa * gate(G_GA, D_ATTN)).astype(BF16), wpa_ref[...])
    hg = hg_ref[0].astype(F32)
    yh = jnp.concatenate(
        [lax.dot_general(hg[:, j * SUBLANES:(j + 1) * SUBLANES, :].reshape(D_HYENA, LANES).astype(BF16),
                         wph_ref[...], TN_DIMS, preferred_element_type=F32) for j in range(tm // LANES)], axis=0)
    mixed = gate(G_MA, D_MODEL) * ya + gate(G_MH, D_MODEL) * yh
    y = x_ref[0] + mod_ref[0, 2:3, :] * _dot(mixed.astype(BF16), wo_ref[...])
    ms = jnp.mean(y * y, axis=-1, keepdims=True)
    out_ref[0] = y * lax.rsqrt(ms + EPS) * fg_ref[...]


def _out_proj(o_pat, lse_pat, gates, hg_ct, x, mod, final_g, wpa, wph, wo, tm):
    B, L, _ = x.shape
    row = lambda width: pl.BlockSpec((1, tm, width), lambda b, i: (b, i, 0))
    strided = lambda r, width: pl.BlockSpec((None, r, tm // r, width), lambda b, i: (b, 0, i, 0))
    const = lambda shape: pl.BlockSpec(shape, lambda b, i: (0,) * len(shape))
    expand = np.zeros((LANES, D_ATTN), np.float32)
    expand[np.arange(D_ATTN) // HEAD_DIM, np.arange(D_ATTN)] = 1.0
    return pl.pallas_call(
        _out_kernel,
        grid=(B, L // tm),
        in_specs=[strided(r, D_ATTN) for r in DILATIONS] + [strided(r, LANES) for r in DILATIONS]
                 + [row(D_GATES),
                    pl.BlockSpec((1, N_CGROUPS, tm // LANES * SUBLANES, LANES), lambda b, i: (b, 0, i, 0)),
                    row(D_MODEL),
                    pl.BlockSpec((1, 3, D_MODEL), lambda b, i: (b, 0, 0)),
                    const((1, D_MODEL)), const((LANES, D_ATTN)),
                    const((D_ATTN, D_MODEL)), const((D_HYENA, D_MODEL)), const((D_MODEL, D_MODEL))],
        out_specs=row(D_MODEL),
        out_shape=jax.ShapeDtypeStruct((B, L, D_MODEL), F32),
        scratch_shapes=[pltpu.VMEM((D_ATTN // LANES, tm, LANES), F32)],
        compiler_params=_params("parallel", "parallel"),
        name="out_proj",
    )(*o_pat, *lse_pat, gates, hg_ct, x, mod, final_g.reshape(1, -1), jnp.asarray(expand, dtype=BF16),
      wpa, wph, wo)


def _trunk(x, mod, wts):
    B, L, _ = x.shape
    assert L % 512 == 0 and B % 2 == 0
    tm = 512
    *qkv, gates, u_ct = _inproj(x, mod, wts["norm_g"], wts["w_rows"], wts["w_hy_t"], tm)
    pats = [_band_attention(t, wts["rel_bias"]) for t in qkv]
    kt_ct = _filters(L, *wts["filt"], tr=512)
    hg_ct = _hyena(u_ct, kt_ct, wts["short_w"], wts["short_b"], wts["hyena_d"], 2 * L // LANES)
    return _out_proj([p[0] for p in pats], [p[1] for p in pats], gates, hg_ct, x, mod, wts["final_g"],
                     wts["w_proj_attn"], wts["w_proj_hyena"], wts["w_out"], tm)


def kernel(x_prompt, x_sample, c_prompt, c_sample, w_ada, b_ada, norm_g, w_in, short_w, short_b, filt_w1, filt_b1,
           filt_freq1, filt_w2, filt_b2, filt_freq2, filt_w3, hyena_d, w_proj_attn, w_proj_hyena, w_out, rel_bias,
           final_g):
    assert w_ada.shape[0] == 1, "single layer"
    bp, bs = c_prompt.shape[0], c_sample.shape[0]
    c_all = jnp.concatenate([c_prompt, c_sample], axis=0)
    rows = -(-(bp + bs) // SUBLANES) * SUBLANES
    c_all = jnp.pad(c_all, ((0, rows - bp - bs), (0, 0)))
    mod = _ada(c_all, w_ada[0], b_ada[0]).reshape(rows, 3, D_MODEL)
    w = w_in[0].astype(BF16)
    wts = dict(norm_g=norm_g[0],
               w_rows=jnp.concatenate([w[:, :C_U], w[:, C_MA:]], axis=1),
               w_hy_t=w[:, C_U:C_MA].T,
               short_w=short_w[0], short_b=short_b[0],
               filt=(filt_w1[0], filt_b1[0], filt_freq1[0], filt_w2[0], filt_b2[0], filt_freq2[0], filt_w3[0]),
               hyena_d=hyena_d[0], w_proj_attn=w_proj_attn[0].astype(BF16),
               w_proj_hyena=w_proj_hyena[0].astype(BF16), w_out=w_out[0].astype(BF16), rel_bias=rel_bias,
               final_g=final_g)
    y_prompt = _trunk(x_prompt, mod[:bp], wts)
    y_sample = _trunk(x_sample, mod[bp:bp + bs], wts)
    return (y_prompt, y_sample)
```

```python
import functools
import math

import numpy as np
import jax
import jax.numpy as jnp
from jax import lax
from jax.experimental import pallas as pl
from jax.experimental.pallas import tpu as pltpu

F32 = jnp.float32
BF16 = jnp.bfloat16

D_MODEL = 1024
N_HEADS = 12
HEAD_DIM = 64
D_ATTN = N_HEADS * HEAD_DIM
N_PAIRS = N_HEADS // 2
DILATIONS = (1, 4, 16)
HALF = 64
N_BUCKETS = 32
MAX_DISTANCE = 1024
D_HYENA = 768
FILTER_EMB = 33
FILTER_HIDDEN = 64
FAST_DECAY_PCT = 0.3
SLOW_DECAY_PCT = 1.5
DECAY_TARGET = 0.01
EPS = 1e-6
NEG_INF = -1e30
LANES = 128
SUBLANES = 8
BF16_ROWS = 16
VMEM_LIMIT = 56 * 1024 * 1024
N_CGROUPS = D_HYENA // SUBLANES

C_Q, C_K, C_V, C_GA = 0, 768, 1536, 2304
C_U = 3072
C_GH = C_U + 3 * D_HYENA
C_MA = C_GH + D_HYENA
C_MH = C_MA + D_MODEL
D_PROJ = C_MH + D_MODEL
R_Q, R_K, R_V, R_GA, R_MA, R_MH = 0, 768, 1536, 2304, 3072, 4096
D_ROWMAJOR = 5120
G_GA, G_MA, G_MH = 0, 768, 1792
D_GATES = 2816
INPROJ_CHUNK = 256

LOG2_E = math.log2(math.e)

NT_DIMS = (((1,), (1,)), ((), ()))
TN_DIMS = (((0,), (0,)), ((), ()))


def _params(*sem):
    return pltpu.CompilerParams(dimension_semantics=sem, vmem_limit_bytes=VMEM_LIMIT)


def _sigmoid(x):
    return 1.0 / (1.0 + jnp.exp(-x))


def _dot(a, b):
    return jnp.dot(a, b, preferred_element_type=F32)


def _to_channel_tiled(ref, lead, value):
    groups = value.shape[0] // SUBLANES
    band = lambda j: value[:, j * LANES:(j + 1) * LANES].reshape(groups, SUBLANES, LANES)
    for j in range(0, value.shape[1] // LANES, 2):
        rows = slice(j * SUBLANES, (j + 2) * SUBLANES)
        ref[lead + (rows, slice(None))] = jnp.concatenate([band(j), band(j + 1)], axis=1).astype(ref.dtype)


def _ada_kernel(c_ref, w_ref, b_ref, o_ref):
    c = c_ref[...]
    s = c * _sigmoid(c)
    o_ref[...] = jnp.dot(s, w_ref[...], preferred_element_type=F32,
                         precision=lax.Precision.HIGHEST) + b_ref[...]


def _ada(c_pad, w_ada, b_ada):
    rows = c_pad.shape[0]
    return pl.pallas_call(
        _ada_kernel,
        grid=(3,),
        in_specs=[pl.BlockSpec((rows, D_MODEL), lambda j: (0, 0)),
                  pl.BlockSpec((D_MODEL, D_MODEL), lambda j: (0, j)),
                  pl.BlockSpec((1, D_MODEL), lambda j: (0, j))],
        out_specs=pl.BlockSpec((rows, D_MODEL), lambda j: (0, j)),
        out_shape=jax.ShapeDtypeStruct((rows, 3 * D_MODEL), F32),
        compiler_params=_params("arbitrary"),
        name="ada",
    )(c_pad, w_ada, b_ada.reshape(1, -1))


def _inproj_kernel(x_ref, mod_ref, g_ref, w_ref, wt_ref, qkv1_ref, qkv4_ref, qkv16_ref, gates_ref, ct_ref, slab_ref,
                   slab4_ref):
    x = x_ref[0]
    tm = x.shape[0]
    ms = jnp.mean(x * x, axis=-1, keepdims=True)
    y = x * lax.rsqrt(ms + EPS) * g_ref[...]
    h = (y * (1.0 + mod_ref[0, 1:2, :]) + mod_ref[0, 0:1, :]).astype(BF16)
    silu = lambda g: g * _sigmoid(g)

    def store_strided(which, scale, c0):
        def store(val):
            val = val * scale if scale != 1.0 else val
            width = val.shape[1]
            dst = slice(c0, c0 + width)
            qkv1_ref[0, which, 0, :, dst] = val.astype(BF16)
            slabs = range(c0 // LANES, (c0 + width) // LANES)
            for t in slabs:
                slab_ref[t] = val[:, t * LANES - c0:(t + 1) * LANES - c0]
            _, r1, r2 = DILATIONS
            step = r2 // r1
            for rho in range(r1):
                planes = [slab_ref.at[t][pl.ds(rho, tm // r1, stride=r1), :] for t in slabs]
                qkv4_ref[0, which, rho, :, dst] = jnp.concatenate(planes, axis=1).astype(BF16)
                for t, plane in zip(slabs, planes):
                    slab4_ref[t, rho * (tm // r1):(rho + 1) * (tm // r1), :] = plane
            for rho in range(r2):
                start = (rho % r1) * (tm // r1) + rho // r1
                planes = [slab4_ref.at[t][pl.ds(start, tm // r2, stride=step), :] for t in slabs]
                qkv16_ref[0, which, rho, :, dst] = jnp.concatenate(planes, axis=1).astype(BF16)
        return store

    def gate_cols(c0, fn):
        def store(val):
            gates_ref[0, :, c0:c0 + val.shape[1]] = fn(val).astype(BF16)
        return store

    def channel_tiled(g0, fn):
        return lambda val: _to_channel_tiled(ct_ref, (0, slice(g0, g0 + val.shape[0] // SUBLANES)), fn(val))

    rows = lambda c0: lambda: _dot(h, w_ref[:, c0:c0 + INPROJ_CHUNK])
    cols = lambda r0: lambda: lax.dot_general(wt_ref[r0:r0 + INPROJ_CHUNK, :], h, NT_DIMS,
                                              preferred_element_type=F32)
    ident = lambda v: v
    chunks = lambda width: range(0, width, INPROJ_CHUNK)
    stages = []
    for which, (r0, scale) in enumerate(((R_Q, LOG2_E / math.sqrt(HEAD_DIM)), (R_K, 1.0), (R_V, 1.0))):
        stages += [(rows(r0 + c0), store_strided(which, scale, c0)) for c0 in chunks(D_ATTN)]
    stages += [(rows(R_GA + c0), gate_cols(G_GA + c0, silu)) for c0 in chunks(D_ATTN)]
    stages += [(rows(R_MA + c0), gate_cols(G_MA + c0, _sigmoid)) for c0 in chunks(2 * D_MODEL)]
    stages += [(cols(r0), channel_tiled(r0 // SUBLANES, ident if r0 < 3 * D_HYENA else silu))
               for r0 in chunks(4 * D_HYENA)]
    pending = stages[0][0]()
    for j, (_, finish) in enumerate(stages):
        nxt = stages[j + 1][0]() if j + 1 < len(stages) else None
        finish(pending)
        pending = nxt


def _inproj(x, mod, norm_g, w_rows, w_hy_t, tm):
    B, L, _ = x.shape
    rows_ct = tm // LANES * SUBLANES
    strided = lambda r: pl.BlockSpec((1, 3, r, tm // r, D_ATTN), lambda b, i: (b, 0, 0, i, 0))
    strided_shape = lambda r: jax.ShapeDtypeStruct((B, 3, r, L // r, D_ATTN), BF16)
    resident = lambda shape: pl.BlockSpec(shape, lambda b, i: (0, 0), pipeline_mode=pl.Buffered(1))
    return pl.pallas_call(
        _inproj_kernel,
        grid=(B, L // tm),
        in_specs=[pl.BlockSpec((1, tm, D_MODEL), lambda b, i: (b, i, 0)),
                  pl.BlockSpec((1, 3, D_MODEL), lambda b, i: (b, 0, 0)),
                  pl.BlockSpec((1, D_MODEL), lambda b, i: (0, 0)),
                  resident((D_MODEL, D_ROWMAJOR)), resident((4 * D_HYENA, D_MODEL))],
        out_specs=[strided(r) for r in DILATIONS]
                  + [pl.BlockSpec((1, tm, D_GATES), lambda b, i: (b, i, 0)),
                     pl.BlockSpec((1, 4 * N_CGROUPS, rows_ct, LANES), lambda b, i: (b, 0, i, 0))],
        out_shape=[strided_shape(r) for r in DILATIONS]
                  + [jax.ShapeDtypeStruct((B, L, D_GATES), BF16),
                     jax.ShapeDtypeStruct((B, 4 * N_CGROUPS, L // LANES * SUBLANES, LANES), BF16)],
        scratch_shapes=[pltpu.VMEM((D_ATTN // LANES, tm, LANES), F32), pltpu.VMEM((D_ATTN // LANES, tm, LANES), F32)],
        compiler_params=_params("parallel", "parallel"),
        name="inproj",
    )(x, mod, norm_g.reshape(1, -1), w_rows, w_hy_t)


def _t5_bucket(rel):
    nb = N_BUCKETS // 2
    max_exact = nb // 2
    n = np.abs(rel)
    large = max_exact + (np.log(np.maximum(n, 1) / max_exact) / math.log(MAX_DISTANCE / max_exact)
                         * (nb - max_exact)).astype(np.int32)
    large = np.minimum(large, nb - 1)
    return ((rel > 0).astype(np.int32) * nb + np.where(n < max_exact, n, large)).astype(np.int32)


def _bias_table(rel_bias, r, qb):
    a = np.arange(qb)[:, None]
    b = np.arange(qb + 2 * HALF)[None, :]
    rel = b - HALF - a
    valid = np.abs(rel) <= HALF
    onehot = (jnp.asarray(_t5_bucket(rel * r))[..., None] == jnp.arange(N_BUCKETS)).astype(F32)
    bias = jnp.dot(onehot, rel_bias.astype(F32), precision=lax.Precision.HIGHEST)
    bias = jnp.where(valid[..., None], bias * LOG2_E, NEG_INF).transpose(2, 0, 1)
    bias = bias.reshape(N_PAIRS, 2 * qb, qb + 2 * HALF)
    left = b < HALF
    right = b >= qb + HALF
    return jnp.stack([bias, jnp.where(left, NEG_INF, bias), jnp.where(right, NEG_INF, bias),
                      jnp.where(left | right, NEG_INF, bias)])


STAT_L_LANE = 16


def _attn_kernel(q_ref, kp_ref, kc_ref, kn_ref, vp_ref, vc_ref, vn_ref, bias_ref, o_ref, stat_ref, *, mb, qb, lr):
    i = pl.program_id(1)
    nsub = mb // qb
    lane = lax.broadcasted_iota(jnp.int32, (qb, LANES), 1)
    lo = lane < HEAD_DIM
    first = (i == 0).astype(jnp.int32)
    last = (i == lr // mb - 1).astype(jnp.int32)

    def window(p_ref, c_ref, n_ref, s, cols):
        parts = []
        if s == 0:
            parts.append(p_ref[:, cols])
        lo_row = max(s * qb - HALF, 0)
        hi_row = min((s + 1) * qb + HALF, mb)
        parts.append(c_ref[lo_row:hi_row, cols])
        if s == nsub - 1:
            parts.append(n_ref[:, cols])
        return parts[0] if len(parts) == 1 else jnp.concatenate(parts, axis=0)

    for s in range(nsub):
        variant = (first if s == 0 else 0) + 2 * (last if s == nsub - 1 else 0)
        rows = slice(s * qb, (s + 1) * qb)
        stat_ref[rows, :] = jnp.zeros((qb, LANES), F32)
        for p in range(N_PAIRS):
            cols = slice(p * LANES, (p + 1) * LANES)
            qp = q_ref[s * qb:(s + 1) * qb, cols]
            zero = jnp.zeros_like(qp)
            q2 = jnp.concatenate([jnp.where(lo, qp, zero), jnp.where(lo, zero, qp)], axis=0)
            kwin = window(kp_ref, kc_ref, kn_ref, s, cols)
            vwin = window(vp_ref, vc_ref, vn_ref, s, cols)
            logits = lax.dot_general(q2, kwin, NT_DIMS, preferred_element_type=F32) + bias_ref[variant, p]
            m = jnp.max(logits, axis=-1, keepdims=True)
            e = jnp.exp2(logits - m)
            l = jnp.sum(e, axis=-1, keepdims=True)
            o = _dot(e.astype(BF16), vwin)
            o_ref[s * qb:(s + 1) * qb, cols] = jnp.where(lo, o[:qb], o[qb:]).astype(BF16)
            for h in range(2):
                stat_ref[rows, 2 * p + h:2 * p + h + 1] = m[h * qb:(h + 1) * qb]
                stat_ref[rows, STAT_L_LANE + 2 * p + h:STAT_L_LANE + 2 * p + h + 1] = l[h * qb:(h + 1) * qb]


def _band_attention(qkv, rel_bias):
    B, _, r, lr, _ = qkv.shape
    qb = 128
    mb = min(512, lr)
    hb = mb // HALF
    nhalo = lr // HALF
    cur_rows = lambda i: i
    prev_rows = lambda i: jnp.maximum(i * hb - 1, 0)
    next_rows = lambda i: jnp.minimum((i + 1) * hb, nhalo - 1)
    part = lambda which, rows, idx: pl.BlockSpec((None, None, None, rows, D_ATTN),
                                                 lambda b, i, rho: (b, which, rho, idx(i), 0))
    out = lambda width: pl.BlockSpec((None, None, mb, width), lambda b, i, rho: (b, rho, i, 0))
    bias = _bias_table(rel_bias, r, qb)
    return pl.pallas_call(
        functools.partial(_attn_kernel, mb=mb, qb=qb, lr=lr),
        grid=(B, lr // mb, r),
        in_specs=[part(0, mb, cur_rows),
                  part(1, HALF, prev_rows), part(1, mb, cur_rows), part(1, HALF, next_rows),
                  part(2, HALF, prev_rows), part(2, mb, cur_rows), part(2, HALF, next_rows),
                  pl.BlockSpec(bias.shape, lambda b, i, rho: (0, 0, 0, 0), pipeline_mode=pl.Buffered(1))],
        out_specs=[out(D_ATTN), out(LANES)],
        out_shape=[jax.ShapeDtypeStruct((B, r, lr, D_ATTN), BF16),
                   jax.ShapeDtypeStruct((B, r, lr, LANES), F32)],
        compiler_params=_params("parallel", "parallel", "parallel"),
        name=f"attn_r{r}",
    )(*([qkv] * 7), bias)


def _filter_features(L):
    j = jnp.arange(2 * L, dtype=jnp.int32)
    pos = jnp.where(j < L, j, 2 * L - j).astype(F32)
    t = (pos / (L - 1))[:, None]
    bands = (FILTER_EMB - 1) // 2
    w = (2.0 * math.pi / L) * pos[:, None]
    f = jnp.linspace(1e-4, bands - 1, bands, dtype=F32)[None, :]
    z = jnp.concatenate([t, jnp.cos(f * w), -jnp.sin(f * w)], axis=-1)
    return jnp.pad(z, ((0, 0), (0, LANES - FILTER_EMB)))


def _filt_kernel(z_ref, w1_ref, w1s_ref, b1_ref, f1_ref, w2_ref, b2_ref, f2_ref, w3a_lo_ref, w3a_hi_ref, w3b_lo_ref,
                 w3b_hi_ref, w3a0_ref, w3b0_ref, dl_ref, o_ref, *, tr, L):
    i = pl.program_id(0)
    half = tr // 2
    pre = _dot(z_ref[0:half, :], w1_ref[...]) + _dot(z_ref[half:tr, :], w1s_ref[...])
    h = jnp.sin(f1_ref[...] * (pre + b1_ref[...]))
    h = jnp.sin(f2_ref[...] * (_dot(h, w2_ref[...]) + b2_ref[...]))
    lane = lax.broadcasted_iota(jnp.int32, (1, LANES), 1)
    for n, (lo_ref, hi_ref, w0_ref) in enumerate(((w3a_lo_ref, w3a_hi_ref, w3a0_ref),
                                                  (w3b_lo_ref, w3b_hi_ref, w3b0_ref))):
        nt = lambda w_ref: lax.dot_general(w_ref[...], h, NT_DIMS, preferred_element_type=F32)
        kt = jnp.concatenate([nt(lo_ref), nt(hi_ref)], axis=1)
        for j in range(tr // LANES):
            col = i * tr + j * LANES + lane
            pos = jnp.where(col < L, col, 2 * L - col).astype(F32)
            decay = jnp.exp(-(pos / (L - 1)) * dl_ref[...])
            slab = jnp.where(col != L, kt[:, j * LANES:(j + 1) * LANES] * decay, 0.0)
            o_ref[n, :, j * SUBLANES:(j + 1) * SUBLANES, :] = slab.reshape(N_CGROUPS, SUBLANES, LANES)

        @pl.when(i == 0)
        def _():
            back = lax.dot_general(w0_ref[...], h[0:LANES], NT_DIMS, preferred_element_type=F32)
            add = jnp.where(lane == 0, back, 0.0).reshape(N_CGROUPS, SUBLANES, LANES)
            o_ref[n, :, 0:SUBLANES, :] = o_ref[n, :, 0:SUBLANES, :] + add


def _filters(L, w1, b1, f1, w2, b2, f2, w3, tr):
    hid = FILTER_HIDDEN
    z = _filter_features(L)
    w1_lo = jnp.pad(w1, ((0, LANES - FILTER_EMB), (0, LANES - hid)))
    w1_hi = jnp.pad(w1, ((0, LANES - FILTER_EMB), (LANES - hid, 0)))
    w2_bd = jnp.pad(w2, ((0, LANES - hid), (0, LANES - hid))) + jnp.pad(w2, ((LANES - hid, 0), (LANES - hid, 0)))
    w3t_lo = jnp.pad(w3, ((0, LANES - hid), (0, 0))).T
    w3t_hi = jnp.pad(w3, ((LANES - hid, 0), (0, 0))).T
    vec = lambda a: jnp.tile(a.reshape(1, -1), (1, LANES // hid))
    deltas = np.abs(np.linspace(math.log(DECAY_TARGET) / SLOW_DECAY_PCT, math.log(DECAY_TARGET) / FAST_DECAY_PCT,
                                D_HYENA)).astype(np.float32)
    deltas = jnp.asarray(np.broadcast_to(deltas[:, None], (D_HYENA, LANES)))
    nt = 2 * L // tr
    half = nt // 2
    const = lambda shape: pl.BlockSpec(shape, lambda i: (0, 0))
    w3spec = lambda n: pl.BlockSpec((D_HYENA, LANES), lambda i: (2 * n + jnp.where(i < half, 0, 1), 0))
    w3back = lambda n: pl.BlockSpec((D_HYENA, LANES), lambda i: (2 * n + 1, 0))
    rows_ct = tr // LANES * SUBLANES
    return pl.pallas_call(
        functools.partial(_filt_kernel, tr=tr, L=L),
        grid=(nt,),
        in_specs=[pl.BlockSpec((tr, LANES), lambda i: (i, 0)),
                  const((LANES, LANES)), const((LANES, LANES)), const((1, LANES)), const((1, LANES)),
                  const((LANES, LANES)), const((1, LANES)), const((1, LANES)),
                  w3spec(0), w3spec(0), w3spec(1), w3spec(1), w3back(0), w3back(1), const((D_HYENA, LANES))],
        out_specs=pl.BlockSpec((2, N_CGROUPS, rows_ct, LANES), lambda i: (0, 0, i, 0)),
        out_shape=jax.ShapeDtypeStruct((2, N_CGROUPS, 2 * L // LANES * SUBLANES, LANES), F32),
        compiler_params=_params("arbitrary"),
        name="hyena_filters",
    )(z, w1_lo, w1_hi, vec(b1), vec(f1), w2_bd, vec(b2), vec(f2), w3t_lo, w3t_hi, w3t_lo, w3t_hi, w3t_lo, w3t_lo,
      deltas)


def _dft_left(n_out, n_in, n, sign):
    a = np.arange(n_out)[:, None]
    b = np.arange(n_in)[None, :]
    ang = 2.0 * np.pi * ((a * b) % n) / n
    c, s = np.cos(ang), sign * np.sin(ang)
    return np.block([[c, -s], [s, c]])


def _dft_right(n, sign):
    return _dft_left(n, n, n, sign).T


def _hyena_kernel(v_ref, x1_ref, x2_ref, gate_ref, kt_ref, sw_ref, sb_ref, d_ref, f1_ref, f1k_ref, f2f_ref, f2i_ref,
                  f3_ref, tc_ref, ts_ref, o_ref, kr_ref, ki_ref, in_ref, out_ref, *, n1):
    n1h = n1 // 2
    n2 = LANES
    nch = SUBLANES
    tc, ts = tc_ref[...], ts_ref[...]
    ts_neg = -ts

    def channel_rows(ref2d, c, n):
        return ref2d[pl.ds(c, n, stride=SUBLANES), :]

    def lanes(mats):
        return jnp.concatenate(mats, axis=1)

    def cmul(ar, ai, br, bi):
        return lanes([ar * br - ai * bi, ar * bi + ai * br])

    def channel_parts(a, c, rows):
        return a[:rows, c * n2:(c + 1) * n2], a[rows:, c * n2:(c + 1) * n2]

    def level2(lhs, mat_ref):
        x = _dot(lhs, mat_ref[...])
        return x[:, :n2], x[:, n2:]

    @pl.when(pl.program_id(1) == 0)
    def _():
        for n in range(2):
            kmat = lanes([channel_rows(kt_ref.at[n, 0], c, n1) for c in range(nch)]).astype(BF16)
            a = _dot(f1k_ref[...], kmat).astype(BF16)
            for c in range(nch):
                ar, ai = channel_parts(a, c, n1)
                xr, xi = level2(cmul(ar, ai, tc, ts_neg), f2f_ref)
                kr_ref[n, c] = (xr * (1.0 / (n1 * n2))).astype(BF16)
                ki_ref[n, c] = (xi * (1.0 / (n1 * n2))).astype(BF16)

    lane = lax.broadcasted_iota(jnp.int32, (n1h, n2), 1)
    row = lax.broadcasted_iota(jnp.int32, (n1h, n2), 0)
    parts = (v_ref, x1_ref, x2_ref, gate_ref)
    for j, ref in enumerate(parts):
        for b in range(2):
            in_ref[j, b] = ref[b, 0].astype(F32)

    def short_conv(part, b, c):
        m = channel_rows(in_ref.at[part, b], c, n1h)
        left = pltpu.roll(m, 1, 1)
        before = jnp.where(lane == 0, jnp.where(row == 0, 0.0, pltpu.roll(left, 1, 0)), left)
        right = pltpu.roll(m, n2 - 1, 1)
        after = jnp.where(lane == n2 - 1, jnp.where(row == n1h - 1, 0.0, pltpu.roll(right, n1h - 1, 0)), right)
        tap = lambda k: sw_ref[k, part, c:c + 1, :]
        return before * tap(0) + m * tap(1) + after * tap(2) + sb_ref[part, c:c + 1, :]

    z = [[short_conv(0, b, c) for c in range(nch)] for b in range(2)]
    for n in range(2):
        stacked = jnp.concatenate([lanes(z[0]), lanes(z[1])], axis=0).astype(BF16)
        a = _dot(f1_ref[...], stacked).astype(BF16)
        rows_of = lambda t, c: t[c * n1:(c + 1) * n1]
        x = _dot(jnp.concatenate([cmul(*channel_parts(a, c, n1), tc, ts_neg) for c in range(nch)], axis=0),
                 f2f_ref[...]).astype(BF16)
        bk = _dot(jnp.concatenate([cmul(rows_of(x, c)[:, :n2], rows_of(x, c)[:, n2:], kr_ref[n, c], ki_ref[n, c])
                                   for c in range(nch)], axis=0), f2i_ref[...]).astype(BF16)
        back = [cmul(rows_of(bk, c)[:, :n2], rows_of(bk, c)[:, n2:], tc, ts) for c in range(nch)]
        brs = [t[:, :n2] for t in back]
        bis = [t[:, n2:] for t in back]
        y = _dot(f3_ref[...], jnp.concatenate([lanes(brs), lanes(bis)], axis=0))
        for b in range(2):
            for c in range(nch):
                conv = y[b * n1h:(b + 1) * n1h, c * n2:(c + 1) * n2] + z[b][c] * d_ref[n, c:c + 1, :]
                nxt = short_conv(n + 1, b, c) * conv
                if n == 0:
                    z[b][c] = nxt
                else:
                    gate = channel_rows(in_ref.at[3, b], c, n1h)
                    out_ref.at[b][pl.ds(c, n1h, stride=SUBLANES), :] = nxt * gate
    for b in range(2):
        o_ref[b, 0] = out_ref[b].astype(o_ref.dtype)


def _hyena(u_ct, kt_ct, short_w, short_b, hyena_d, n1):
    B, _, R, _ = u_ct.shape
    n1h, n2 = n1 // 2, LANES
    bf = lambda m: jnp.asarray(m, dtype=BF16)
    f1 = bf(_dft_left(n1, n1h, n1, -1))
    f1k = bf(_dft_left(n1, n1, n1, -1)[:, :n1])
    f2f, f2i = bf(_dft_right(n2, -1)), bf(_dft_right(n2, +1))
    f3 = bf(_dft_left(n1h, n1, n1, +1))
    ang = 2.0 * np.pi * ((np.arange(n1)[:, None] * np.arange(n2)[None, :]) % (n1 * n2)) / (n1 * n2)
    tc, ts = bf(np.cos(ang)), bf(np.sin(ang))
    wide = lambda t: jnp.broadcast_to(t[..., None], t.shape + (LANES,))
    sw = wide(short_w).reshape(3, 3, N_CGROUPS, SUBLANES, LANES)
    sb = wide(short_b).reshape(3, N_CGROUPS, SUBLANES, LANES)
    dd = wide(hyena_d).reshape(2, N_CGROUPS, SUBLANES, LANES)

    part = lambda j: pl.BlockSpec((2, 1, R, LANES), lambda g, p: (p, g + j * N_CGROUPS, 0, 0))
    const = lambda a: pl.BlockSpec(a.shape, lambda g, p: (0,) * a.ndim)
    mats = (f1, f1k, f2f, f2i, f3, tc, ts)
    return pl.pallas_call(
        functools.partial(_hyena_kernel, n1=n1),
        grid=(N_CGROUPS, B // 2),
        in_specs=[part(0), part(1), part(2), part(3),
                  pl.BlockSpec((2, 1, 2 * R, LANES), lambda g, p: (0, g, 0, 0)),
                  pl.BlockSpec((3, 3, None, SUBLANES, LANES), lambda g, p: (0, 0, g, 0, 0)),
                  pl.BlockSpec((3, None, SUBLANES, LANES), lambda g, p: (0, g, 0, 0)),
                  pl.BlockSpec((2, None, SUBLANES, LANES), lambda g, p: (0, g, 0, 0))]
                 + [const(m) for m in mats],
        out_specs=pl.BlockSpec((2, 1, R, LANES), lambda g, p: (p, g, 0, 0)),
        out_shape=jax.ShapeDtypeStruct((B, N_CGROUPS, R, LANES), BF16),
        scratch_shapes=[pltpu.VMEM((2, SUBLANES, n1, n2), BF16), pltpu.VMEM((2, SUBLANES, n1, n2), BF16),
                        pltpu.VMEM((4, 2, R, LANES), F32), pltpu.VMEM((2, R, LANES), F32)],
        compiler_params=_params("parallel", "arbitrary"),
        name="hyena",
    )(u_ct, u_ct, u_ct, u_ct, kt_ct, sw, sb, dd, *mats)


def _out_kernel(o1_ref, o4_ref, o16_ref, l1_ref, l4_ref, l16_ref, gates_ref, hg_ref, x_ref, mod_ref,
                fg_ref, e_ref, wpa_ref, wph_ref, wo_ref, out_ref, slab_ref, slab4_ref):
    tm = x_ref.shape[1]

    def token_order(ref, r):
        if r == 1:
            return ref[0].astype(F32)
        _, r1, r2 = DILATIONS
        nslab = ref.shape[-1] // LANES
        part = lambda rho, t: ref[rho, :, t * LANES:(t + 1) * LANES].astype(F32)
        if r == r1:
            planes = lambda rho, t: part(rho, t)
        else:
            step = r2 // r1
            for rho in range(r2):
                start = (rho % r1) * (tm // r1) + rho // r1
                for t in range(nslab):
                    slab4_ref.at[t][pl.ds(start, tm // r2, stride=step), :] = part(rho, t)
            planes = lambda rho, t: slab4_ref[t, rho * (tm // r1):(rho + 1) * (tm // r1), :]
        for rho in range(r1):
            for t in range(nslab):
                slab_ref.at[t][pl.ds(rho, tm // r1, stride=r1), :] = planes(rho, t)
        return jnp.concatenate([slab_ref[t] for t in range(nslab)], axis=1)

    stats = [token_order(ref, r) for ref, r in zip((l1_ref, l4_ref, l16_ref), DILATIONS)]
    mx = jnp.maximum(jnp.maximum(stats[0], stats[1]), stats[2])
    w = [jnp.exp2(st - mx) for st in stats]
    rowsum = [pltpu.roll(st, LANES - STAT_L_LANE, 1) for st in stats]
    tot = w[0] * rowsum[0] + w[1] * rowsum[1] + w[2] * rowsum[2]
    head_lane = lax.broadcasted_iota(jnp.int32, mx.shape, 1) < N_HEADS
    heads = lambda a: _dot(jnp.where(head_lane, a / tot, 0.0).astype(BF16), e_ref[...])
    oa = heads(w[0]) * token_order(o1_ref, 1)
    oa = oa + heads(w[1]) * token_order(o4_ref, 4)
    oa = oa + heads(w[2]) * token_order(o16_ref, 16)
    gate = lambda c0, width: gates_ref[0, :, c0:c0 + width].astype(F32)
    ya = _dot((oa * gate(G_GA, D_ATTN)).astype(BF16), wpa_ref[...])
    hg = hg_ref[0].astype(F32)
    yh = jnp.concatenate(
        [lax.dot_general(hg[:, j * SUBLANES:(j + 1) * SUBLANES, :].reshape(D_HYENA, LANES).astype(BF16),
                         wph_ref[...], TN_DIMS, preferred_element_type=F32) for j in range(tm // LANES)], axis=0)
    mixed = gate(G_MA, D_MODEL) * ya + gate(G_MH, D_MODEL) * yh
    y = x_ref[0] + mod_ref[0, 2:3, :] * _dot(mixed.astype(BF16), wo_ref[...])
    ms = jnp.mean(y * y, axis=-1, keepdims=True)
    out_ref[0] = y * lax.rsqrt(ms + EPS) * fg_ref[...]


def _out_proj(o_pat, lse_pat, gates, hg_ct, x, mod, final_g, wpa, wph, wo, tm):
    B, L, _ = x.shape
    row = lambda width: pl.BlockSpec((1, tm, width), lambda b, i: (b, i, 0))
    strided = lambda r, width: pl.BlockSpec((None, r, tm // r, width), lambda b, i: (b, 0, i, 0))
    const = lambda shape: pl.BlockSpec(shape, lambda b, i: (0,) * len(shape))
    expand = np.zeros((LANES, D_ATTN), np.float32)
    expand[np.arange(D_ATTN) // HEAD_DIM, np.arange(D_ATTN)] = 1.0
    return pl.pallas_call(
        _out_kernel,
        grid=(B, L // tm),
        in_specs=[strided(r, D_ATTN) for r in DILATIONS] + [strided(r, LANES) for r in DILATIONS]
                 + [row(D_GATES),
                    pl.BlockSpec((1, N_CGROUPS, tm // LANES * SUBLANES, LANES), lambda b, i: (b, 0, i, 0)),
                    row(D_MODEL),
                    pl.BlockSpec((1, 3, D_MODEL), lambda b, i: (b, 0, 0)),
                    const((1, D_MODEL)), const((LANES, D_ATTN)),
                    const((D_ATTN, D_MODEL)), const((D_HYENA, D_MODEL)), const((D_MODEL, D_MODEL))],
        out_specs=row(D_MODEL),
        out_shape=jax.ShapeDtypeStruct((B, L, D_MODEL), F32),
        scratch_shapes=[pltpu.VMEM((D_ATTN // LANES, tm, LANES), F32), pltpu.VMEM((D_ATTN // LANES, tm, LANES), F32)],
        compiler_params=_params("parallel", "parallel"),
        name="out_proj",
    )(*o_pat, *lse_pat, gates, hg_ct, x, mod, final_g.reshape(1, -1), jnp.asarray(expand, dtype=BF16),
      wpa, wph, wo)


def _trunk(x, mod, wts):
    B, L, _ = x.shape
    assert L % 512 == 0 and B % 2 == 0
    tm = 512
    *qkv, gates, u_ct = _inproj(x, mod, wts["norm_g"], wts["w_rows"], wts["w_hy_t"], tm)
    pats = [_band_attention(t, wts["rel_bias"]) for t in qkv]
    kt_ct = _filters(L, *wts["filt"], tr=512)
    hg_ct = _hyena(u_ct, kt_ct, wts["short_w"], wts["short_b"], wts["hyena_d"], 2 * L // LANES)
    return _out_proj([p[0] for p in pats], [p[1] for p in pats], gates, hg_ct, x, mod, wts["final_g"],
                     wts["w_proj_attn"], wts["w_proj_hyena"], wts["w_out"], tm)


def kernel(x_prompt, x_sample, c_prompt, c_sample, w_ada, b_ada, norm_g, w_in, short_w, short_b, filt_w1, filt_b1,
           filt_freq1, filt_w2, filt_b2, filt_freq2, filt_w3, hyena_d, w_proj_attn, w_proj_hyena, w_out, rel_bias,
           final_g):
    assert w_ada.shape[0] == 1, "single layer"
    bp, bs = c_prompt.shape[0], c_sample.shape[0]
    c_all = jnp.concatenate([c_prompt, c_sample], axis=0)
    rows = -(-(bp + bs) // SUBLANES) * SUBLANES
    c_all = jnp.pad(c_all, ((0, rows - bp - bs), (0, 0)))
    mod = _ada(c_all, w_ada[0], b_ada[0]).reshape(rows, 3, D_MODEL)
    w = w_in[0].astype(BF16)
    wts = dict(norm_g=norm_g[0],
               w_rows=jnp.concatenate([w[:, :C_U], w[:, C_MA:]], axis=1),
               w_hy_t=w[:, C_U:C_MA].T,
               short_w=short_w[0], short_b=short_b[0],
               filt=(filt_w1[0], filt_b1[0], filt_freq1[0], filt_w2[0], filt_b2[0], filt_freq2[0], filt_w3[0]),
               hyena_d=hyena_d[0], w_proj_attn=w_proj_attn[0].astype(BF16),
               w_proj_hyena=w_proj_hyena[0].astype(BF16), w_out=w_out[0].astype(BF16), rel_bias=rel_bias,
               final_g=final_g)
    y_prompt = _trunk(x_prompt, mod[:bp], wts)
    y_sample = _trunk(x_sample, mod[bp:bp + bs], wts)
    return (y_prompt, y_sample)
```

```python
import functools
import math

import numpy as np
import jax
import jax.numpy as jnp
from jax import lax
from jax.experimental import pallas as pl
from jax.experimental.pallas import tpu as pltpu

F32 = jnp.float32
BF16 = jnp.bfloat16

D_MODEL = 1024
N_HEADS = 12
HEAD_DIM = 64
D_ATTN = N_HEADS * HEAD_DIM
N_PAIRS = N_HEADS // 2
DILATIONS = (1, 4, 16)
HALF = 64
N_BUCKETS = 32
MAX_DISTANCE = 1024
D_HYENA = 768
FILTER_EMB = 33
FILTER_HIDDEN = 64
FAST_DECAY_PCT = 0.3
SLOW_DECAY_PCT = 1.5
DECAY_TARGET = 0.01
EPS = 1e-6
NEG_INF = -1e30
LANES = 128
SUBLANES = 8
BF16_ROWS = 16
VMEM_LIMIT = 56 * 1024 * 1024
N_CGROUPS = D_HYENA // SUBLANES

C_Q, C_K, C_V, C_GA = 0, 768, 1536, 2304
C_U = 3072
C_GH = C_U + 3 * D_HYENA
C_MA = C_GH + D_HYENA
C_MH = C_MA + D_MODEL
D_PROJ = C_MH + D_MODEL
R_Q, R_K, R_V, R_GA, R_MA, R_MH = 0, 768, 1536, 2304, 3072, 4096
D_ROWMAJOR = 5120
G_GA, G_MA, G_MH = 0, 768, 1792
D_GATES = 2816
INPROJ_CHUNK = 256

LOG2_E = math.log2(math.e)

NT_DIMS = (((1,), (1,)), ((), ()))
TN_DIMS = (((0,), (0,)), ((), ()))


def _params(*sem):
    return pltpu.CompilerParams(dimension_semantics=sem, vmem_limit_bytes=VMEM_LIMIT)


def _sigmoid(x):
    return 1.0 / (1.0 + jnp.exp(-x))


def _dot(a, b):
    return jnp.dot(a, b, preferred_element_type=F32)


def _to_channel_tiled(ref, lead, value):
    groups = value.shape[0] // SUBLANES
    band = lambda j: value[:, j * LANES:(j + 1) * LANES].reshape(groups, SUBLANES, LANES)
    for j in range(0, value.shape[1] // LANES, 2):
        rows = slice(j * SUBLANES, (j + 2) * SUBLANES)
        ref[lead + (rows, slice(None))] = jnp.concatenate([band(j), band(j + 1)], axis=1).astype(ref.dtype)


def _ada_kernel(c_ref, w_ref, b_ref, o_ref):
    c = c_ref[...]
    s = c * _sigmoid(c)
    o_ref[...] = jnp.dot(s, w_ref[...], preferred_element_type=F32,
                         precision=lax.Precision.HIGHEST) + b_ref[...]


def _ada(c_pad, w_ada, b_ada):
    rows = c_pad.shape[0]
    return pl.pallas_call(
        _ada_kernel,
        grid=(3,),
        in_specs=[pl.BlockSpec((rows, D_MODEL), lambda j: (0, 0)),
                  pl.BlockSpec((D_MODEL, D_MODEL), lambda j: (0, j)),
                  pl.BlockSpec((1, D_MODEL), lambda j: (0, j))],
        out_specs=pl.BlockSpec((rows, D_MODEL), lambda j: (0, j)),
        out_shape=jax.ShapeDtypeStruct((rows, 3 * D_MODEL), F32),
        compiler_params=_params("arbitrary"),
        name="ada",
    )(c_pad, w_ada, b_ada.reshape(1, -1))


def _inproj_kernel(x_ref, mod_ref, g_ref, w_ref, wt_ref, qkv1_ref, qkv4_ref, qkv16_ref, gates_ref, ct_ref, slab_ref,
                   slab4_ref):
    x = x_ref[0]
    tm = x.shape[0]
    ms = jnp.mean(x * x, axis=-1, keepdims=True)
    y = x * lax.rsqrt(ms + EPS) * g_ref[...]
    h = (y * (1.0 + mod_ref[0, 1:2, :]) + mod_ref[0, 0:1, :]).astype(BF16)
    silu = lambda g: g * _sigmoid(g)

    def store_strided(which, scale, c0):
        def store(val):
            val = val * scale if scale != 1.0 else val
            width = val.shape[1]
            dst = slice(c0, c0 + width)
            qkv1_ref[0, which, 0, :, dst] = val.astype(BF16)
            slabs = range(c0 // LANES, (c0 + width) // LANES)
            for t in slabs:
                slab_ref[t] = val[:, t * LANES - c0:(t + 1) * LANES - c0]
            _, r1, r2 = DILATIONS
            step = r2 // r1
            for rho in range(r1):
                planes = [slab_ref.at[t][pl.ds(rho, tm // r1, stride=r1), :] for t in slabs]
                qkv4_ref[0, which, rho, :, dst] = jnp.concatenate(planes, axis=1).astype(BF16)
                for t, plane in zip(slabs, planes):
                    slab4_ref[t, rho * (tm // r1):(rho + 1) * (tm // r1), :] = plane
            for rho in range(r2):
                start = (rho % r1) * (tm // r1) + rho // r1
                planes = [slab4_ref.at[t][pl.ds(start, tm // r2, stride=step), :] for t in slabs]
                qkv16_ref[0, which, rho, :, dst] = jnp.concatenate(planes, axis=1).astype(BF16)
        return store

    def gate_cols(c0, fn):
        def store(val):
            gates_ref[0, :, c0:c0 + val.shape[1]] = fn(val).astype(BF16)
        return store

    def channel_tiled(g0, fn):
        return lambda val: _to_channel_tiled(ct_ref, (0, slice(g0, g0 + val.shape[0] // SUBLANES)), fn(val))

    rows = lambda c0: lambda: _dot(h, w_ref[:, c0:c0 + INPROJ_CHUNK])
    cols = lambda r0: lambda: lax.dot_general(wt_ref[r0:r0 + INPROJ_CHUNK, :], h, NT_DIMS,
                                              preferred_element_type=F32)
    ident = lambda v: v
    chunks = lambda width: range(0, width, INPROJ_CHUNK)
    stages = []
    for which, (r0, scale) in enumerate(((R_Q, LOG2_E / math.sqrt(HEAD_DIM)), (R_K, 1.0), (R_V, 1.0))):
        stages += [(rows(r0 + c0), store_strided(which, scale, c0)) for c0 in chunks(D_ATTN)]
    stages += [(rows(R_GA + c0), gate_cols(G_GA + c0, silu)) for c0 in chunks(D_ATTN)]
    stages += [(rows(R_MA + c0), gate_cols(G_MA + c0, _sigmoid)) for c0 in chunks(2 * D_MODEL)]
    stages += [(cols(r0), channel_tiled(r0 // SUBLANES, ident if r0 < 3 * D_HYENA else silu))
               for r0 in chunks(4 * D_HYENA)]
    pending = stages[0][0]()
    for j, (_, finish) in enumerate(stages):
        nxt = stages[j + 1][0]() if j + 1 < len(stages) else None
        finish(pending)
        pending = nxt


def _inproj(x, mod, norm_g, w_rows, w_hy_t, tm):
    B, L, _ = x.shape
    rows_ct = tm // LANES * SUBLANES
    strided = lambda r: pl.BlockSpec((1, 3, r, tm // r, D_ATTN), lambda b, i: (b, 0, 0, i, 0))
    strided_shape = lambda r: jax.ShapeDtypeStruct((B, 3, r, L // r, D_ATTN), BF16)
    resident = lambda shape: pl.BlockSpec(shape, lambda b, i: (0, 0), pipeline_mode=pl.Buffered(1))
    return pl.pallas_call(
        _inproj_kernel,
        grid=(B, L // tm),
        in_specs=[pl.BlockSpec((1, tm, D_MODEL), lambda b, i: (b, i, 0)),
                  pl.BlockSpec((1, 3, D_MODEL), lambda b, i: (b, 0, 0)),
                  pl.BlockSpec((1, D_MODEL), lambda b, i: (0, 0)),
                  resident((D_MODEL, D_ROWMAJOR)), resident((4 * D_HYENA, D_MODEL))],
        out_specs=[strided(r) for r in DILATIONS]
                  + [pl.BlockSpec((1, tm, D_GATES), lambda b, i: (b, i, 0)),
                     pl.BlockSpec((1, 4 * N_CGROUPS, rows_ct, LANES), lambda b, i: (b, 0, i, 0))],
        out_shape=[strided_shape(r) for r in DILATIONS]
                  + [jax.ShapeDtypeStruct((B, L, D_GATES), BF16),
                     jax.ShapeDtypeStruct((B, 4 * N_CGROUPS, L // LANES * SUBLANES, LANES), BF16)],
        scratch_shapes=[pltpu.VMEM((D_ATTN // LANES, tm, LANES), F32), pltpu.VMEM((D_ATTN // LANES, tm, LANES), F32)],
        compiler_params=_params("parallel", "parallel"),
        name="inproj",
    )(x, mod, norm_g.reshape(1, -1), w_rows, w_hy_t)


def _t5_bucket(rel):
    nb = N_BUCKETS // 2
    max_exact = nb // 2
    n = np.abs(rel)
    large = max_exact + (np.log(np.maximum(n, 1) / max_exact) / math.log(MAX_DISTANCE / max_exact)
                         * (nb - max_exact)).astype(np.int32)
    large = np.minimum(large, nb - 1)
    return ((rel > 0).astype(np.int32) * nb + np.where(n < max_exact, n, large)).astype(np.int32)


def _bias_table(rel_bias, r, qb):
    a = np.arange(qb)[:, None]
    b = np.arange(qb + 2 * HALF)[None, :]
    rel = b - HALF - a
    valid = np.abs(rel) <= HALF
    onehot = (jnp.asarray(_t5_bucket(rel * r))[..., None] == jnp.arange(N_BUCKETS)).astype(F32)
    bias = jnp.dot(onehot, rel_bias.astype(F32), precision=lax.Precision.HIGHEST)
    bias = jnp.where(valid[..., None], bias * LOG2_E, NEG_INF).transpose(2, 0, 1)
    bias = bias.reshape(N_PAIRS, 2 * qb, qb + 2 * HALF)
    left = b < HALF
    right = b >= qb + HALF
    return jnp.stack([bias, jnp.where(left, NEG_INF, bias), jnp.where(right, NEG_INF, bias),
                      jnp.where(left | right, NEG_INF, bias)])


STAT_L_LANE = 16


def _attn_kernel(q_ref, kp_ref, kc_ref, kn_ref, vp_ref, vc_ref, vn_ref, bias_ref, o_ref, stat_ref, *, mb, qb, lr):
    i = pl.program_id(1)
    nsub = mb // qb
    lane = lax.broadcasted_iota(jnp.int32, (qb, LANES), 1)
    lo = lane < HEAD_DIM
    first = (i == 0).astype(jnp.int32)
    last = (i == lr // mb - 1).astype(jnp.int32)

    def window(p_ref, c_ref, n_ref, s, cols):
        parts = []
        if s == 0:
            parts.append(p_ref[:, cols])
        lo_row = max(s * qb - HALF, 0)
        hi_row = min((s + 1) * qb + HALF, mb)
        parts.append(c_ref[lo_row:hi_row, cols])
        if s == nsub - 1:
            parts.append(n_ref[:, cols])
        return parts[0] if len(parts) == 1 else jnp.concatenate(parts, axis=0)

    for s in range(nsub):
        variant = (first if s == 0 else 0) + 2 * (last if s == nsub - 1 else 0)
        rows = slice(s * qb, (s + 1) * qb)
        stat_ref[rows, :] = jnp.zeros((qb, LANES), F32)
        for p in range(N_PAIRS):
            cols = slice(p * LANES, (p + 1) * LANES)
            qp = q_ref[s * qb:(s + 1) * qb, cols]
            zero = jnp.zeros_like(qp)
            q2 = jnp.concatenate([jnp.where(lo, qp, zero), jnp.where(lo, zero, qp)], axis=0)
            kwin = window(kp_ref, kc_ref, kn_ref, s, cols)
            vwin = window(vp_ref, vc_ref, vn_ref, s, cols)
            logits = lax.dot_general(q2, kwin, NT_DIMS, preferred_element_type=F32) + bias_ref[variant, p]
            m = jnp.max(logits, axis=-1, keepdims=True)
            e = jnp.exp2(logits - m)
            l = jnp.sum(e, axis=-1, keepdims=True)
            o = _dot(e.astype(BF16), vwin)
            o_ref[s * qb:(s + 1) * qb, cols] = jnp.where(lo, o[:qb], o[qb:]).astype(BF16)
            for h in range(2):
                stat_ref[rows, 2 * p + h:2 * p + h + 1] = m[h * qb:(h + 1) * qb]
                stat_ref[rows, STAT_L_LANE + 2 * p + h:STAT_L_LANE + 2 * p + h + 1] = l[h * qb:(h + 1) * qb]


def _band_attention(qkv, rel_bias):
    B, _, r, lr, _ = qkv.shape
    qb = 128
    mb = min(2048, lr)
    hb = mb // HALF
    nhalo = lr // HALF
    cur_rows = lambda i: i
    prev_rows = lambda i: jnp.maximum(i * hb - 1, 0)
    next_rows = lambda i: jnp.minimum((i + 1) * hb, nhalo - 1)
    part = lambda which, rows, idx: pl.BlockSpec((None, None, None, rows, D_ATTN),
                                                 lambda b, i, rho: (b, which, rho, idx(i), 0))
    out = lambda width: pl.BlockSpec((None, None, mb, width), lambda b, i, rho: (b, rho, i, 0))
    bias = _bias_table(rel_bias, r, qb)
    return pl.pallas_call(
        functools.partial(_attn_kernel, mb=mb, qb=qb, lr=lr),
        grid=(B, lr // mb, r),
        in_specs=[part(0, mb, cur_rows),
                  part(1, HALF, prev_rows), part(1, mb, cur_rows), part(1, HALF, next_rows),
                  part(2, HALF, prev_rows), part(2, mb, cur_rows), part(2, HALF, next_rows),
                  pl.BlockSpec(bias.shape, lambda b, i, rho: (0, 0, 0, 0), pipeline_mode=pl.Buffered(1))],
        out_specs=[out(D_ATTN), out(LANES)],
        out_shape=[jax.ShapeDtypeStruct((B, r, lr, D_ATTN), BF16),
                   jax.ShapeDtypeStruct((B, r, lr, LANES), F32)],
        compiler_params=_params("parallel", "parallel", "parallel"),
        name=f"attn_r{r}",
    )(*([qkv] * 7), bias)


def _filter_features(L):
    j = jnp.arange(2 * L, dtype=jnp.int32)
    pos = jnp.where(j < L, j, 2 * L - j).astype(F32)
    t = (pos / (L - 1))[:, None]
    bands = (FILTER_EMB - 1) // 2
    w = (2.0 * math.pi / L) * pos[:, None]
    f = jnp.linspace(1e-4, bands - 1, bands, dtype=F32)[None, :]
    z = jnp.concatenate([t, jnp.cos(f * w), -jnp.sin(f * w)], axis=-1)
    return jnp.pad(z, ((0, 0), (0, LANES - FILTER_EMB)))


def _filt_kernel(z_ref, w1_ref, w1s_ref, b1_ref, f1_ref, w2_ref, b2_ref, f2_ref, w3a_lo_ref, w3a_hi_ref, w3b_lo_ref,
                 w3b_hi_ref, w3a0_ref, w3b0_ref, dl_ref, o_ref, *, tr, L):
    i = pl.program_id(0)
    half = tr // 2
    pre = _dot(z_ref[0:half, :], w1_ref[...]) + _dot(z_ref[half:tr, :], w1s_ref[...])
    h = jnp.sin(f1_ref[...] * (pre + b1_ref[...]))
    h = jnp.sin(f2_ref[...] * (_dot(h, w2_ref[...]) + b2_ref[...]))
    lane = lax.broadcasted_iota(jnp.int32, (1, LANES), 1)
    for n, (lo_ref, hi_ref, w0_ref) in enumerate(((w3a_lo_ref, w3a_hi_ref, w3a0_ref),
                                                  (w3b_lo_ref, w3b_hi_ref, w3b0_ref))):
        nt = lambda w_ref: lax.dot_general(w_ref[...], h, NT_DIMS, preferred_element_type=F32)
        kt = jnp.concatenate([nt(lo_ref), nt(hi_ref)], axis=1)
        for j in range(tr // LANES):
            col = i * tr + j * LANES + lane
            pos = jnp.where(col < L, col, 2 * L - col).astype(F32)
            decay = jnp.exp(-(pos / (L - 1)) * dl_ref[...])
            slab = jnp.where(col != L, kt[:, j * LANES:(j + 1) * LANES] * decay, 0.0)
            o_ref[n, :, j * SUBLANES:(j + 1) * SUBLANES, :] = slab.reshape(N_CGROUPS, SUBLANES, LANES)

        @pl.when(i == 0)
        def _():
            back = lax.dot_general(w0_ref[...], h[0:LANES], NT_DIMS, preferred_element_type=F32)
            add = jnp.where(lane == 0, back, 0.0).reshape(N_CGROUPS, SUBLANES, LANES)
            o_ref[n, :, 0:SUBLANES, :] = o_ref[n, :, 0:SUBLANES, :] + add


def _filters(L, w1, b1, f1, w2, b2, f2, w3, tr):
    hid = FILTER_HIDDEN
    z = _filter_features(L)
    w1_lo = jnp.pad(w1, ((0, LANES - FILTER_EMB), (0, LANES - hid)))
    w1_hi = jnp.pad(w1, ((0, LANES - FILTER_EMB), (LANES - hid, 0)))
    w2_bd = jnp.pad(w2, ((0, LANES - hid), (0, LANES - hid))) + jnp.pad(w2, ((LANES - hid, 0), (LANES - hid, 0)))
    w3t_lo = jnp.pad(w3, ((0, LANES - hid), (0, 0))).T
    w3t_hi = jnp.pad(w3, ((LANES - hid, 0), (0, 0))).T
    vec = lambda a: jnp.tile(a.reshape(1, -1), (1, LANES // hid))
    deltas = np.abs(np.linspace(math.log(DECAY_TARGET) / SLOW_DECAY_PCT, math.log(DECAY_TARGET) / FAST_DECAY_PCT,
                                D_HYENA)).astype(np.float32)
    deltas = jnp.asarray(np.broadcast_to(deltas[:, None], (D_HYENA, LANES)))
    nt = 2 * L // tr
    half = nt // 2
    const = lambda shape: pl.BlockSpec(shape, lambda i: (0, 0))
    w3spec = lambda n: pl.BlockSpec((D_HYENA, LANES), lambda i: (2 * n + jnp.where(i < half, 0, 1), 0))
    w3back = lambda n: pl.BlockSpec((D_HYENA, LANES), lambda i: (2 * n + 1, 0))
    rows_ct = tr // LANES * SUBLANES
    return pl.pallas_call(
        functools.partial(_filt_kernel, tr=tr, L=L),
        grid=(nt,),
        in_specs=[pl.BlockSpec((tr, LANES), lambda i: (i, 0)),
                  const((LANES, LANES)), const((LANES, LANES)), const((1, LANES)), const((1, LANES)),
                  const((LANES, LANES)), const((1, LANES)), const((1, LANES)),
                  w3spec(0), w3spec(0), w3spec(1), w3spec(1), w3back(0), w3back(1), const((D_HYENA, LANES))],
        out_specs=pl.BlockSpec((2, N_CGROUPS, rows_ct, LANES), lambda i: (0, 0, i, 0)),
        out_shape=jax.ShapeDtypeStruct((2, N_CGROUPS, 2 * L // LANES * SUBLANES, LANES), F32),
        compiler_params=_params("arbitrary"),
        name="hyena_filters",
    )(z, w1_lo, w1_hi, vec(b1), vec(f1), w2_bd, vec(b2), vec(f2), w3t_lo, w3t_hi, w3t_lo, w3t_hi, w3t_lo, w3t_lo,
      deltas)


def _dft_left(n_out, n_in, n, sign):
    a = np.arange(n_out)[:, None]
    b = np.arange(n_in)[None, :]
    ang = 2.0 * np.pi * ((a * b) % n) / n
    c, s = np.cos(ang), sign * np.sin(ang)
    return np.block([[c, -s], [s, c]])


def _dft_right(n, sign):
    return _dft_left(n, n, n, sign).T


def _hyena_kernel(v_ref, x1_ref, x2_ref, gate_ref, kt_ref, sw_ref, sb_ref, d_ref, f1_ref, f1k_ref, f2f_ref, f2i_ref,
                  f3_ref, tc_ref, ts_ref, o_ref, kr_ref, ki_ref, in_ref, out_ref, *, n1):
    n1h = n1 // 2
    n2 = LANES
    nch = SUBLANES
    tc, ts = tc_ref[...], ts_ref[...]
    ts_neg = -ts

    def channel_rows(ref2d, c, n):
        return ref2d[pl.ds(c, n, stride=SUBLANES), :]

    def lanes(mats):
        return jnp.concatenate(mats, axis=1)

    def cmul(ar, ai, br, bi):
        return lanes([ar * br - ai * bi, ar * bi + ai * br])

    def channel_parts(a, c, rows):
        return a[:rows, c * n2:(c + 1) * n2], a[rows:, c * n2:(c + 1) * n2]

    def level2(lhs, mat_ref):
        x = _dot(lhs, mat_ref[...])
        return x[:, :n2], x[:, n2:]

    @pl.when(pl.program_id(1) == 0)
    def _():
        for n in range(2):
            kmat = lanes([channel_rows(kt_ref.at[n, 0], c, n1) for c in range(nch)]).astype(BF16)
            a = _dot(f1k_ref[...], kmat).astype(BF16)
            for c in range(nch):
                ar, ai = channel_parts(a, c, n1)
                xr, xi = level2(cmul(ar, ai, tc, ts_neg), f2f_ref)
                kr_ref[n, c] = (xr * (1.0 / (n1 * n2))).astype(BF16)
                ki_ref[n, c] = (xi * (1.0 / (n1 * n2))).astype(BF16)

    lane = lax.broadcasted_iota(jnp.int32, (n1h, n2), 1)
    row = lax.broadcasted_iota(jnp.int32, (n1h, n2), 0)
    parts = (v_ref, x1_ref, x2_ref, gate_ref)
    for j, ref in enumerate(parts):
        for b in range(2):
            in_ref[j, b] = ref[b, 0].astype(F32)

    def short_conv(part, b, c):
        m = channel_rows(in_ref.at[part, b], c, n1h)
        left = pltpu.roll(m, 1, 1)
        before = jnp.where(lane == 0, jnp.where(row == 0, 0.0, pltpu.roll(left, 1, 0)), left)
        right = pltpu.roll(m, n2 - 1, 1)
        after = jnp.where(lane == n2 - 1, jnp.where(row == n1h - 1, 0.0, pltpu.roll(right, n1h - 1, 0)), right)
        tap = lambda k: sw_ref[k, part, c:c + 1, :]
        return before * tap(0) + m * tap(1) + after * tap(2) + sb_ref[part, c:c + 1, :]

    z = [[short_conv(0, b, c) for c in range(nch)] for b in range(2)]
    for n in range(2):
        stacked = jnp.concatenate([lanes(z[0]), lanes(z[1])], axis=0).astype(BF16)
        a = _dot(f1_ref[...], stacked).astype(BF16)
        rows_of = lambda t, c: t[c * n1:(c + 1) * n1]
        x = _dot(jnp.concatenate([cmul(*channel_parts(a, c, n1), tc, ts_neg) for c in range(nch)], axis=0),
                 f2f_ref[...]).astype(BF16)
        bk = _dot(jnp.concatenate([cmul(rows_of(x, c)[:, :n2], rows_of(x, c)[:, n2:], kr_ref[n, c], ki_ref[n, c])
                                   for c in range(nch)], axis=0), f2i_ref[...]).astype(BF16)
        back = [cmul(rows_of(bk, c)[:, :n2], rows_of(bk, c)[:, n2:], tc, ts) for c in range(nch)]
        brs = [t[:, :n2] for t in back]
        bis = [t[:, n2:] for t in back]
        y = _dot(f3_ref[...], jnp.concatenate([lanes(brs), lanes(bis)], axis=0))
        for b in range(2):
            for c in range(nch):
                conv = y[b * n1h:(b + 1) * n1h, c * n2:(c + 1) * n2] + z[b][c] * d_ref[n, c:c + 1, :]
                nxt = short_conv(n + 1, b, c) * conv
                if n == 0:
                    z[b][c] = nxt
                else:
                    gate = channel_rows(in_ref.at[3, b], c, n1h)
                    out_ref.at[b][pl.ds(c, n1h, stride=SUBLANES), :] = nxt * gate
    for b in range(2):
        o_ref[b, 0] = out_ref[b].astype(o_ref.dtype)


def _hyena(u_ct, kt_ct, short_w, short_b, hyena_d, n1):
    B, _, R, _ = u_ct.shape
    n1h, n2 = n1 // 2, LANES
    bf = lambda m: jnp.asarray(m, dtype=BF16)
    f1 = bf(_dft_left(n1, n1h, n1, -1))
    f1k = bf(_dft_left(n1, n1, n1, -1)[:, :n1])
    f2f, f2i = bf(_dft_right(n2, -1)), bf(_dft_right(n2, +1))
    f3 = bf(_dft_left(n1h, n1, n1, +1))
    ang = 2.0 * np.pi * ((np.arange(n1)[:, None] * np.arange(n2)[None, :]) % (n1 * n2)) / (n1 * n2)
    tc, ts = bf(np.cos(ang)), bf(np.sin(ang))
    wide = lambda t: jnp.broadcast_to(t[..., None], t.shape + (LANES,))
    sw = wide(short_w).reshape(3, 3, N_CGROUPS, SUBLANES, LANES)
    sb = wide(short_b).reshape(3, N_CGROUPS, SUBLANES, LANES)
    dd = wide(hyena_d).reshape(2, N_CGROUPS, SUBLANES, LANES)

    part = lambda j: pl.BlockSpec((2, 1, R, LANES), lambda g, p: (p, g + j * N_CGROUPS, 0, 0))
    const = lambda a: pl.BlockSpec(a.shape, lambda g, p: (0,) * a.ndim)
    mats = (f1, f1k, f2f, f2i, f3, tc, ts)
    return pl.pallas_call(
        functools.partial(_hyena_kernel, n1=n1),
        grid=(N_CGROUPS, B // 2),
        in_specs=[part(0), part(1), part(2), part(3),
                  pl.BlockSpec((2, 1, 2 * R, LANES), lambda g, p: (0, g, 0, 0)),
                  pl.BlockSpec((3, 3, None, SUBLANES, LANES), lambda g, p: (0, 0, g, 0, 0)),
                  pl.BlockSpec((3, None, SUBLANES, LANES), lambda g, p: (0, g, 0, 0)),
                  pl.BlockSpec((2, None, SUBLANES, LANES), lambda g, p: (0, g, 0, 0))]
                 + [const(m) for m in mats],
        out_specs=pl.BlockSpec((2, 1, R, LANES), lambda g, p: (p, g, 0, 0)),
        out_shape=jax.ShapeDtypeStruct((B, N_CGROUPS, R, LANES), BF16),
        scratch_shapes=[pltpu.VMEM((2, SUBLANES, n1, n2), BF16), pltpu.VMEM((2, SUBLANES, n1, n2), BF16),
                        pltpu.VMEM((4, 2, R, LANES), F32), pltpu.VMEM((2, R, LANES), F32)],
        compiler_params=_params("parallel", "arbitrary"),
        name="hyena",
    )(u_ct, u_ct, u_ct, u_ct, kt_ct, sw, sb, dd, *mats)


def _out_kernel(o1_ref, o4_ref, o16_ref, l1_ref, l4_ref, l16_ref, gates_ref, hg_ref, x_ref, mod_ref,
                fg_ref, e_ref, wpa_ref, wph_ref, wo_ref, out_ref, slab_ref, slab4_ref):
    tm = x_ref.shape[1]

    def token_order(ref, r):
        if r == 1:
            return ref[0].astype(F32)
        _, r1, r2 = DILATIONS
        nslab = ref.shape[-1] // LANES
        part = lambda rho, t: ref[rho, :, t * LANES:(t + 1) * LANES].astype(F32)
        if r == r1:
            planes = lambda rho, t: part(rho, t)
        else:
            step = r2 // r1
            for rho in range(r2):
                start = (rho % r1) * (tm // r1) + rho // r1
                for t in range(nslab):
                    slab4_ref.at[t][pl.ds(start, tm // r2, stride=step), :] = part(rho, t)
            planes = lambda rho, t: slab4_ref[t, rho * (tm // r1):(rho + 1) * (tm // r1), :]
        for rho in range(r1):
            for t in range(nslab):
                slab_ref.at[t][pl.ds(rho, tm // r1, stride=r1), :] = planes(rho, t)
        return jnp.concatenate([slab_ref[t] for t in range(nslab)], axis=1)

    stats = [token_order(ref, r) for ref, r in zip((l1_ref, l4_ref, l16_ref), DILATIONS)]
    mx = jnp.maximum(jnp.maximum(stats[0], stats[1]), stats[2])
    w = [jnp.exp2(st - mx) for st in stats]
    rowsum = [pltpu.roll(st, LANES - STAT_L_LANE, 1) for st in stats]
    tot = w[0] * rowsum[0] + w[1] * rowsum[1] + w[2] * rowsum[2]
    head_lane = lax.broadcasted_iota(jnp.int32, mx.shape, 1) < N_HEADS
    heads = lambda a: _dot(jnp.where(head_lane, a / tot, 0.0).astype(BF16), e_ref[...])
    oa = heads(w[0]) * token_order(o1_ref, 1)
    oa = oa + heads(w[1]) * token_order(o4_ref, 4)
    oa = oa + heads(w[2]) * token_order(o16_ref, 16)
    gate = lambda c0, width: gates_ref[0, :, c0:c0 + width].astype(F32)
    ya = _dot((oa * gate(G_GA, D_ATTN)).astype(BF16), wpa_ref[...])
    hg = hg_ref[0].astype(F32)
    yh = jnp.concatenate(
        [lax.dot_general(hg[:, j * SUBLANES:(j + 1) * SUBLANES, :].reshape(D_HYENA, LANES).astype(BF16),
                         wph_ref[...], TN_DIMS, preferred_element_type=F32) for j in range(tm // LANES)], axis=0)
    mixed = gate(G_MA, D_MODEL) * ya + gate(G_MH, D_MODEL) * yh
    y = x_ref[0] + mod_ref[0, 2:3, :] * _dot(mixed.astype(BF16), wo_ref[...])
    ms = jnp.mean(y * y, axis=-1, keepdims=True)
    out_ref[0] = y * lax.rsqrt(ms + EPS) * fg_ref[...]


def _out_proj(o_pat, lse_pat, gates, hg_ct, x, mod, final_g, wpa, wph, wo, tm):
    B, L, _ = x.shape
    row = lambda width: pl.BlockSpec((1, tm, width), lambda b, i: (b, i, 0))
    strided = lambda r, width: pl.BlockSpec((None, r, tm // r, width), lambda b, i: (b, 0, i, 0))
    const = lambda shape: pl.BlockSpec(shape, lambda b, i: (0,) * len(shape))
    expand = np.zeros((LANES, D_ATTN), np.float32)
    expand[np.arange(D_ATTN) // HEAD_DIM, np.arange(D_ATTN)] = 1.0
    return pl.pallas_call(
        _out_kernel,
        grid=(B, L // tm),
        in_specs=[strided(r, D_ATTN) for r in DILATIONS] + [strided(r, LANES) for r in DILATIONS]
                 + [row(D_GATES),
                    pl.BlockSpec((1, N_CGROUPS, tm // LANES * SUBLANES, LANES), lambda b, i: (b, 0, i, 0)),
                    row(D_MODEL),
                    pl.BlockSpec((1, 3, D_MODEL), lambda b, i: (b, 0, 0)),
                    const((1, D_MODEL)), const((LANES, D_ATTN)),
                    const((D_ATTN, D_MODEL)), const((D_HYENA, D_MODEL)), const((D_MODEL, D_MODEL))],
        out_specs=row(D_MODEL),
        out_shape=jax.ShapeDtypeStruct((B, L, D_MODEL), F32),
        scratch_shapes=[pltpu.VMEM((D_ATTN // LANES, tm, LANES), F32), pltpu.VMEM((D_ATTN // LANES, tm, LANES), F32)],
        compiler_params=_params("parallel", "parallel"),
        name="out_proj",
    )(*o_pat, *lse_pat, gates, hg_ct, x, mod, final_g.reshape(1, -1), jnp.asarray(expand, dtype=BF16),
      wpa, wph, wo)


def _trunk(x, mod, wts):
    B, L, _ = x.shape
    assert L % 512 == 0 and B % 2 == 0
    tm = 512
    *qkv, gates, u_ct = _inproj(x, mod, wts["norm_g"], wts["w_rows"], wts["w_hy_t"], tm)
    pats = [_band_attention(t, wts["rel_bias"]) for t in qkv]
    kt_ct = _filters(L, *wts["filt"], tr=512)
    hg_ct = _hyena(u_ct, kt_ct, wts["short_w"], wts["short_b"], wts["hyena_d"], 2 * L // LANES)
    return _out_proj([p[0] for p in pats], [p[1] for p in pats], gates, hg_ct, x, mod, wts["final_g"],
                     wts["w_proj_attn"], wts["w_proj_hyena"], wts["w_out"], tm)


def kernel(x_prompt, x_sample, c_prompt, c_sample, w_ada, b_ada, norm_g, w_in, short_w, short_b, filt_w1, filt_b1,
           filt_freq1, filt_w2, filt_b2, filt_freq2, filt_w3, hyena_d, w_proj_attn, w_proj_hyena, w_out, rel_bias,
           final_g):
    assert w_ada.shape[0] == 1, "single layer"
    bp, bs = c_prompt.shape[0], c_sample.shape[0]
    c_all = jnp.concatenate([c_prompt, c_sample], axis=0)
    rows = -(-(bp + bs) // SUBLANES) * SUBLANES
    c_all = jnp.pad(c_all, ((0, rows - bp - bs), (0, 0)))
    mod = _ada(c_all, w_ada[0], b_ada[0]).reshape(rows, 3, D_MODEL)
    w = w_in[0].astype(BF16)
    wts = dict(norm_g=norm_g[0],
               w_rows=jnp.concatenate([w[:, :C_U], w[:, C_MA:]], axis=1),
               w_hy_t=w[:, C_U:C_MA].T,
               short_w=short_w[0], short_b=short_b[0],
               filt=(filt_w1[0], filt_b1[0], filt_freq1[0], filt_w2[0], filt_b2[0], filt_freq2[0], filt_w3[0]),
               hyena_d=hyena_d[0], w_proj_attn=w_proj_attn[0].astype(BF16),
               w_proj_hyena=w_proj_hyena[0].astype(BF16), w_out=w_out[0].astype(BF16), rel_bias=rel_bias,
               final_g=final_g)
    y_prompt = _trunk(x_prompt, mod[:bp], wts)
    y_sample = _trunk(x_sample, mod[bp:bp + bs], wts)
    return (y_prompt, y_sample)
```

```python
import functools
import math

import numpy as np
import jax
import jax.numpy as jnp
from jax import lax
from jax.experimental import pallas as pl
from jax.experimental.pallas import tpu as pltpu

F32 = jnp.float32
BF16 = jnp.bfloat16

D_MODEL = 1024
N_HEADS = 12
HEAD_DIM = 64
D_ATTN = N_HEADS * HEAD_DIM
N_PAIRS = N_HEADS // 2
DILATIONS = (1, 4, 16)
HALF = 64
N_BUCKETS = 32
MAX_DISTANCE = 1024
D_HYENA = 768
FILTER_EMB = 33
FILTER_HIDDEN = 64
FAST_DECAY_PCT = 0.3
SLOW_DECAY_PCT = 1.5
DECAY_TARGET = 0.01
EPS = 1e-6
NEG_INF = -1e30
LANES = 128
SUBLANES = 8
BF16_ROWS = 16
VMEM_LIMIT = 56 * 1024 * 1024
N_CGROUPS = D_HYENA // SUBLANES

C_Q, C_K, C_V, C_GA = 0, 768, 1536, 2304
C_U = 3072
C_GH = C_U + 3 * D_HYENA
C_MA = C_GH + D_HYENA
C_MH = C_MA + D_MODEL
D_PROJ = C_MH + D_MODEL
R_Q, R_K, R_V, R_GA, R_MA, R_MH = 0, 768, 1536, 2304, 3072, 4096
D_ROWMAJOR = 5120
G_GA, G_MA, G_MH = 0, 768, 1792
D_GATES = 2816
INPROJ_CHUNK = 256

LOG2_E = math.log2(math.e)

NT_DIMS = (((1,), (1,)), ((), ()))
TN_DIMS = (((0,), (0,)), ((), ()))


def _params(*sem):
    return pltpu.CompilerParams(dimension_semantics=sem, vmem_limit_bytes=VMEM_LIMIT)


def _sigmoid(x):
    return 1.0 / (1.0 + jnp.exp(-x))


def _dot(a, b):
    return jnp.dot(a, b, preferred_element_type=F32)


def _to_channel_tiled(ref, lead, value):
    groups = value.shape[0] // SUBLANES
    band = lambda j: value[:, j * LANES:(j + 1) * LANES].reshape(groups, SUBLANES, LANES)
    for j in range(0, value.shape[1] // LANES, 2):
        rows = slice(j * SUBLANES, (j + 2) * SUBLANES)
        ref[lead + (rows, slice(None))] = jnp.concatenate([band(j), band(j + 1)], axis=1).astype(ref.dtype)


def _ada_kernel(c_ref, w_ref, b_ref, o_ref):
    c = c_ref[...]
    s = c * _sigmoid(c)
    o_ref[...] = jnp.dot(s, w_ref[...], preferred_element_type=F32,
                         precision=lax.Precision.HIGHEST) + b_ref[...]


def _ada(c_pad, w_ada, b_ada):
    rows = c_pad.shape[0]
    return pl.pallas_call(
        _ada_kernel,
        grid=(3,),
        in_specs=[pl.BlockSpec((rows, D_MODEL), lambda j: (0, 0)),
                  pl.BlockSpec((D_MODEL, D_MODEL), lambda j: (0, j)),
                  pl.BlockSpec((1, D_MODEL), lambda j: (0, j))],
        out_specs=pl.BlockSpec((rows, D_MODEL), lambda j: (0, j)),
        out_shape=jax.ShapeDtypeStruct((rows, 3 * D_MODEL), F32),
        compiler_params=_params("arbitrary"),
        name="ada",
    )(c_pad, w_ada, b_ada.reshape(1, -1))


def _inproj_kernel(x_ref, mod_ref, g_ref, w_ref, wt_ref, qkv1_ref, qkv4_ref, qkv16_ref, gates_ref, ct_ref, slab_ref,
                   slab4_ref):
    x = x_ref[0]
    tm = x.shape[0]
    ms = jnp.mean(x * x, axis=-1, keepdims=True)
    y = x * lax.rsqrt(ms + EPS) * g_ref[...]
    h = (y * (1.0 + mod_ref[0, 1:2, :]) + mod_ref[0, 0:1, :]).astype(BF16)
    silu = lambda g: g * _sigmoid(g)

    def store_strided(which, scale, c0):
        def store(val):
            val = val * scale if scale != 1.0 else val
            width = val.shape[1]
            dst = slice(c0, c0 + width)
            qkv1_ref[0, which, 0, :, dst] = val.astype(BF16)
            slabs = range(c0 // LANES, (c0 + width) // LANES)
            for t in slabs:
                slab_ref[t] = val[:, t * LANES - c0:(t + 1) * LANES - c0]
            _, r1, r2 = DILATIONS
            step = r2 // r1
            for rho in range(r1):
                planes = [slab_ref.at[t][pl.ds(rho, tm // r1, stride=r1), :] for t in slabs]
                qkv4_ref[0, which, rho, :, dst] = jnp.concatenate(planes, axis=1).astype(BF16)
                for t, plane in zip(slabs, planes):
                    slab4_ref[t, rho * (tm // r1):(rho + 1) * (tm // r1), :] = plane
            for rho in range(r2):
                start = (rho % r1) * (tm // r1) + rho // r1
                planes = [slab4_ref.at[t][pl.ds(start, tm // r2, stride=step), :] for t in slabs]
                qkv16_ref[0, which, rho, :, dst] = jnp.concatenate(planes, axis=1).astype(BF16)
        return store

    def gate_cols(c0, fn):
        def store(val):
            gates_ref[0, :, c0:c0 + val.shape[1]] = fn(val).astype(BF16)
        return store

    def channel_tiled(g0, fn):
        return lambda val: _to_channel_tiled(ct_ref, (0, slice(g0, g0 + val.shape[0] // SUBLANES)), fn(val))

    rows = lambda c0: lambda: _dot(h, w_ref[:, c0:c0 + INPROJ_CHUNK])
    cols = lambda r0: lambda: lax.dot_general(wt_ref[r0:r0 + INPROJ_CHUNK, :], h, NT_DIMS,
                                              preferred_element_type=F32)
    ident = lambda v: v
    chunks = lambda width: range(0, width, INPROJ_CHUNK)
    stages = []
    for which, (r0, scale) in enumerate(((R_Q, LOG2_E / math.sqrt(HEAD_DIM)), (R_K, 1.0), (R_V, 1.0))):
        stages += [(rows(r0 + c0), store_strided(which, scale, c0)) for c0 in chunks(D_ATTN)]
    stages += [(rows(R_GA + c0), gate_cols(G_GA + c0, silu)) for c0 in chunks(D_ATTN)]
    stages += [(rows(R_MA + c0), gate_cols(G_MA + c0, _sigmoid)) for c0 in chunks(2 * D_MODEL)]
    stages += [(cols(r0), channel_tiled(r0 // SUBLANES, ident if r0 < 3 * D_HYENA else silu))
               for r0 in chunks(4 * D_HYENA)]
    for compute, finish in stages:
        finish(compute())


def _inproj(x, mod, norm_g, w_rows, w_hy_t, tm):
    B, L, _ = x.shape
    rows_ct = tm // LANES * SUBLANES
    strided = lambda r: pl.BlockSpec((1, 3, r, tm // r, D_ATTN), lambda b, i: (b, 0, 0, i, 0))
    strided_shape = lambda r: jax.ShapeDtypeStruct((B, 3, r, L // r, D_ATTN), BF16)
    resident = lambda shape: pl.BlockSpec(shape, lambda b, i: (0, 0), pipeline_mode=pl.Buffered(1))
    return pl.pallas_call(
        _inproj_kernel,
        grid=(B, L // tm),
        in_specs=[pl.BlockSpec((1, tm, D_MODEL), lambda b, i: (b, i, 0)),
                  pl.BlockSpec((1, 3, D_MODEL), lambda b, i: (b, 0, 0)),
                  pl.BlockSpec((1, D_MODEL), lambda b, i: (0, 0)),
                  resident((D_MODEL, D_ROWMAJOR)), resident((4 * D_HYENA, D_MODEL))],
        out_specs=[strided(r) for r in DILATIONS]
                  + [pl.BlockSpec((1, tm, D_GATES), lambda b, i: (b, i, 0)),
                     pl.BlockSpec((1, 4 * N_CGROUPS, rows_ct, LANES), lambda b, i: (b, 0, i, 0))],
        out_shape=[strided_shape(r) for r in DILATIONS]
                  + [jax.ShapeDtypeStruct((B, L, D_GATES), BF16),
                     jax.ShapeDtypeStruct((B, 4 * N_CGROUPS, L // LANES * SUBLANES, LANES), BF16)],
        scratch_shapes=[pltpu.VMEM((D_ATTN // LANES, tm, LANES), F32), pltpu.VMEM((D_ATTN // LANES, tm, LANES), F32)],
        compiler_params=_params("parallel", "parallel"),
        name="inproj",
    )(x, mod, norm_g.reshape(1, -1), w_rows, w_hy_t)


def _t5_bucket(rel):
    nb = N_BUCKETS // 2
    max_exact = nb // 2
    n = np.abs(rel)
    large = max_exact + (np.log(np.maximum(n, 1) / max_exact) / math.log(MAX_DISTANCE / max_exact)
                         * (nb - max_exact)).astype(np.int32)
    large = np.minimum(large, nb - 1)
    return ((rel > 0).astype(np.int32) * nb + np.where(n < max_exact, n, large)).astype(np.int32)


def _bias_table(rel_bias, r, qb):
    a = np.arange(qb)[:, None]
    b = np.arange(qb + 2 * HALF)[None, :]
    rel = b - HALF - a
    valid = np.abs(rel) <= HALF
    onehot = (jnp.asarray(_t5_bucket(rel * r))[..., None] == jnp.arange(N_BUCKETS)).astype(F32)
    bias = jnp.dot(onehot, rel_bias.astype(F32), precision=lax.Precision.HIGHEST)
    bias = jnp.where(valid[..., None], bias * LOG2_E, NEG_INF).transpose(2, 0, 1)
    bias = bias.reshape(N_PAIRS, 2 * qb, qb + 2 * HALF)
    left = b < HALF
    right = b >= qb + HALF
    return jnp.stack([bias, jnp.where(left, NEG_INF, bias), jnp.where(right, NEG_INF, bias),
                      jnp.where(left | right, NEG_INF, bias)])


STAT_L_LANE = 16


def _attn_kernel(q_ref, kp_ref, kc_ref, kn_ref, vp_ref, vc_ref, vn_ref, bias_ref, o_ref, stat_ref, *, mb, qb, lr):
    i = pl.program_id(1)
    nsub = mb // qb
    lane = lax.broadcasted_iota(jnp.int32, (qb, LANES), 1)
    lo = lane < HEAD_DIM
    first = (i == 0).astype(jnp.int32)
    last = (i == lr // mb - 1).astype(jnp.int32)

    def window(p_ref, c_ref, n_ref, s, cols):
        parts = []
        if s == 0:
            parts.append(p_ref[:, cols])
        lo_row = max(s * qb - HALF, 0)
        hi_row = min((s + 1) * qb + HALF, mb)
        parts.append(c_ref[lo_row:hi_row, cols])
        if s == nsub - 1:
            parts.append(n_ref[:, cols])
        return parts[0] if len(parts) == 1 else jnp.concatenate(parts, axis=0)

    for s in range(nsub):
        variant = (first if s == 0 else 0) + 2 * (last if s == nsub - 1 else 0)
        rows = slice(s * qb, (s + 1) * qb)
        stat_ref[rows, :] = jnp.zeros((qb, LANES), F32)
        for p in range(N_PAIRS):
            cols = slice(p * LANES, (p + 1) * LANES)
            qp = q_ref[s * qb:(s + 1) * qb, cols]
            zero = jnp.zeros_like(qp)
            q2 = jnp.concatenate([jnp.where(lo, qp, zero), jnp.where(lo, zero, qp)], axis=0)
            kwin = window(kp_ref, kc_ref, kn_ref, s, cols)
            vwin = window(vp_ref, vc_ref, vn_ref, s, cols)
            logits = lax.dot_general(q2, kwin, NT_DIMS, preferred_element_type=F32) + bias_ref[variant, p]
            m = jnp.max(logits, axis=-1, keepdims=True)
            e = jnp.exp2(logits - m)
            l = jnp.sum(e, axis=-1, keepdims=True)
            o = _dot(e.astype(BF16), vwin)
            o_ref[s * qb:(s + 1) * qb, cols] = jnp.where(lo, o[:qb], o[qb:]).astype(BF16)
            for h in range(2):
                stat_ref[rows, 2 * p + h:2 * p + h + 1] = m[h * qb:(h + 1) * qb]
                stat_ref[rows, STAT_L_LANE + 2 * p + h:STAT_L_LANE + 2 * p + h + 1] = l[h * qb:(h + 1) * qb]


def _band_attention(qkv, rel_bias):
    B, _, r, lr, _ = qkv.shape
    qb = 128
    mb = min(2048, lr)
    hb = mb // HALF
    nhalo = lr // HALF
    cur_rows = lambda i: i
    prev_rows = lambda i: jnp.maximum(i * hb - 1, 0)
    next_rows = lambda i: jnp.minimum((i + 1) * hb, nhalo - 1)
    part = lambda which, rows, idx: pl.BlockSpec((None, None, None, rows, D_ATTN),
                                                 lambda b, i, rho: (b, which, rho, idx(i), 0))
    out = lambda width: pl.BlockSpec((None, None, mb, width), lambda b, i, rho: (b, rho, i, 0))
    bias = _bias_table(rel_bias, r, qb)
    return pl.pallas_call(
        functools.partial(_attn_kernel, mb=mb, qb=qb, lr=lr),
        grid=(B, lr // mb, r),
        in_specs=[part(0, mb, cur_rows),
                  part(1, HALF, prev_rows), part(1, mb, cur_rows), part(1, HALF, next_rows),
                  part(2, HALF, prev_rows), part(2, mb, cur_rows), part(2, HALF, next_rows),
                  pl.BlockSpec(bias.shape, lambda b, i, rho: (0, 0, 0, 0), pipeline_mode=pl.Buffered(1))],
        out_specs=[out(D_ATTN), out(LANES)],
        out_shape=[jax.ShapeDtypeStruct((B, r, lr, D_ATTN), BF16),
                   jax.ShapeDtypeStruct((B, r, lr, LANES), F32)],
        compiler_params=_params("parallel", "parallel", "parallel"),
        name=f"attn_r{r}",
    )(*([qkv] * 7), bias)


def _filter_features(L):
    j = jnp.arange(2 * L, dtype=jnp.int32)
    pos = jnp.where(j < L, j, 2 * L - j).astype(F32)
    t = (pos / (L - 1))[:, None]
    bands = (FILTER_EMB - 1) // 2
    w = (2.0 * math.pi / L) * pos[:, None]
    f = jnp.linspace(1e-4, bands - 1, bands, dtype=F32)[None, :]
    z = jnp.concatenate([t, jnp.cos(f * w), -jnp.sin(f * w)], axis=-1)
    return jnp.pad(z, ((0, 0), (0, LANES - FILTER_EMB)))


def _filt_kernel(z_ref, w1_ref, w1s_ref, b1_ref, f1_ref, w2_ref, b2_ref, f2_ref, w3a_lo_ref, w3a_hi_ref, w3b_lo_ref,
                 w3b_hi_ref, w3a0_ref, w3b0_ref, dl_ref, o_ref, *, tr, L):
    i = pl.program_id(0)
    half = tr // 2
    pre = _dot(z_ref[0:half, :], w1_ref[...]) + _dot(z_ref[half:tr, :], w1s_ref[...])
    h = jnp.sin(f1_ref[...] * (pre + b1_ref[...]))
    h = jnp.sin(f2_ref[...] * (_dot(h, w2_ref[...]) + b2_ref[...]))
    lane = lax.broadcasted_iota(jnp.int32, (1, LANES), 1)
    for n, (lo_ref, hi_ref, w0_ref) in enumerate(((w3a_lo_ref, w3a_hi_ref, w3a0_ref),
                                                  (w3b_lo_ref, w3b_hi_ref, w3b0_ref))):
        nt = lambda w_ref: lax.dot_general(w_ref[...], h, NT_DIMS, preferred_element_type=F32)
        kt = jnp.concatenate([nt(lo_ref), nt(hi_ref)], axis=1)
        for j in range(tr // LANES):
            col = i * tr + j * LANES + lane
            pos = jnp.where(col < L, col, 2 * L - col).astype(F32)
            decay = jnp.exp(-(pos / (L - 1)) * dl_ref[...])
            slab = jnp.where(col != L, kt[:, j * LANES:(j + 1) * LANES] * decay, 0.0)
            o_ref[n, :, j * SUBLANES:(j + 1) * SUBLANES, :] = slab.reshape(N_CGROUPS, SUBLANES, LANES)

        @pl.when(i == 0)
        def _():
            back = lax.dot_general(w0_ref[...], h[0:LANES], NT_DIMS, preferred_element_type=F32)
            add = jnp.where(lane == 0, back, 0.0).reshape(N_CGROUPS, SUBLANES, LANES)
            o_ref[n, :, 0:SUBLANES, :] = o_ref[n, :, 0:SUBLANES, :] + add


def _filters(L, w1, b1, f1, w2, b2, f2, w3, tr):
    hid = FILTER_HIDDEN
    z = _filter_features(L)
    w1_lo = jnp.pad(w1, ((0, LANES - FILTER_EMB), (0, LANES - hid)))
    w1_hi = jnp.pad(w1, ((0, LANES - FILTER_EMB), (LANES - hid, 0)))
    w2_bd = jnp.pad(w2, ((0, LANES - hid), (0, LANES - hid))) + jnp.pad(w2, ((LANES - hid, 0), (LANES - hid, 0)))
    w3t_lo = jnp.pad(w3, ((0, LANES - hid), (0, 0))).T
    w3t_hi = jnp.pad(w3, ((LANES - hid, 0), (0, 0))).T
    vec = lambda a: jnp.tile(a.reshape(1, -1), (1, LANES // hid))
    deltas = np.abs(np.linspace(math.log(DECAY_TARGET) / SLOW_DECAY_PCT, math.log(DECAY_TARGET) / FAST_DECAY_PCT,
                                D_HYENA)).astype(np.float32)
    deltas = jnp.asarray(np.broadcast_to(deltas[:, None], (D_HYENA, LANES)))
    nt = 2 * L // tr
    half = nt // 2
    const = lambda shape: pl.BlockSpec(shape, lambda i: (0, 0))
    w3spec = lambda n: pl.BlockSpec((D_HYENA, LANES), lambda i: (2 * n + jnp.where(i < half, 0, 1), 0))
    w3back = lambda n: pl.BlockSpec((D_HYENA, LANES), lambda i: (2 * n + 1, 0))
    rows_ct = tr // LANES * SUBLANES
    return pl.pallas_call(
        functools.partial(_filt_kernel, tr=tr, L=L),
        grid=(nt,),
        in_specs=[pl.BlockSpec((tr, LANES), lambda i: (i, 0)),
                  const((LANES, LANES)), const((LANES, LANES)), const((1, LANES)), const((1, LANES)),
                  const((LANES, LANES)), const((1, LANES)), const((1, LANES)),
                  w3spec(0), w3spec(0), w3spec(1), w3spec(1), w3back(0), w3back(1), const((D_HYENA, LANES))],
        out_specs=pl.BlockSpec((2, N_CGROUPS, rows_ct, LANES), lambda i: (0, 0, i, 0)),
        out_shape=jax.ShapeDtypeStruct((2, N_CGROUPS, 2 * L // LANES * SUBLANES, LANES), F32),
        compiler_params=_params("arbitrary"),
        name="hyena_filters",
    )(z, w1_lo, w1_hi, vec(b1), vec(f1), w2_bd, vec(b2), vec(f2), w3t_lo, w3t_hi, w3t_lo, w3t_hi, w3t_lo, w3t_lo,
      deltas)


def _dft_left(n_out, n_in, n, sign):
    a = np.arange(n_out)[:, None]
    b = np.arange(n_in)[None, :]
    ang = 2.0 * np.pi * ((a * b) % n) / n
    c, s = np.cos(ang), sign * np.sin(ang)
    return np.block([[c, -s], [s, c]])


def _dft_right(n, sign):
    return _dft_left(n, n, n, sign).T


def _hyena_kernel(v_ref, x1_ref, x2_ref, gate_ref, kt_ref, sw_ref, sb_ref, d_ref, f1_ref, f1k_ref, f2f_ref, f2i_ref,
                  f3_ref, tc_ref, ts_ref, o_ref, kr_ref, ki_ref, in_ref, out_ref, *, n1):
    n1h = n1 // 2
    n2 = LANES
    nch = SUBLANES
    tc, ts = tc_ref[...], ts_ref[...]
    ts_neg = -ts

    def channel_rows(ref2d, c, n):
        return ref2d[pl.ds(c, n, stride=SUBLANES), :]

    def lanes(mats):
        return jnp.concatenate(mats, axis=1)

    def cmul(ar, ai, br, bi):
        return lanes([ar * br - ai * bi, ar * bi + ai * br])

    def channel_parts(a, c, rows):
        return a[:rows, c * n2:(c + 1) * n2], a[rows:, c * n2:(c + 1) * n2]

    def level2(lhs, mat_ref):
        x = _dot(lhs, mat_ref[...])
        return x[:, :n2], x[:, n2:]

    @pl.when(pl.program_id(1) == 0)
    def _():
        for n in range(2):
            kmat = lanes([channel_rows(kt_ref.at[n, 0], c, n1) for c in range(nch)]).astype(BF16)
            a = _dot(f1k_ref[...], kmat).astype(BF16)
            for c in range(nch):
                ar, ai = channel_parts(a, c, n1)
                xr, xi = level2(cmul(ar, ai, tc, ts_neg), f2f_ref)
                kr_ref[n, c] = (xr * (1.0 / (n1 * n2))).astype(BF16)
                ki_ref[n, c] = (xi * (1.0 / (n1 * n2))).astype(BF16)

    lane = lax.broadcasted_iota(jnp.int32, (n1h, n2), 1)
    row = lax.broadcasted_iota(jnp.int32, (n1h, n2), 0)
    parts = (v_ref, x1_ref, x2_ref, gate_ref)
    for j, ref in enumerate(parts):
        for b in range(2):
            in_ref[j, b] = ref[b, 0].astype(F32)

    def short_conv(part, b, c):
        m = channel_rows(in_ref.at[part, b], c, n1h)
        left = pltpu.roll(m, 1, 1)
        before = jnp.where(lane == 0, jnp.where(row == 0, 0.0, pltpu.roll(left, 1, 0)), left)
        right = pltpu.roll(m, n2 - 1, 1)
        after = jnp.where(lane == n2 - 1, jnp.where(row == n1h - 1, 0.0, pltpu.roll(right, n1h - 1, 0)), right)
        tap = lambda k: sw_ref[k, part, c:c + 1, :]
        return before * tap(0) + m * tap(1) + after * tap(2) + sb_ref[part, c:c + 1, :]

    z = [[short_conv(0, b, c) for c in range(nch)] for b in range(2)]
    for n in range(2):
        stacked = jnp.concatenate([lanes(z[0]), lanes(z[1])], axis=0).astype(BF16)
        a = _dot(f1_ref[...], stacked).astype(BF16)
        rows_of = lambda t, c: t[c * n1:(c + 1) * n1]
        x = _dot(jnp.concatenate([cmul(*channel_parts(a, c, n1), tc, ts_neg) for c in range(nch)], axis=0),
                 f2f_ref[...]).astype(BF16)
        bk = _dot(jnp.concatenate([cmul(rows_of(x, c)[:, :n2], rows_of(x, c)[:, n2:], kr_ref[n, c], ki_ref[n, c])
                                   for c in range(nch)], axis=0), f2i_ref[...]).astype(BF16)
        back = [cmul(rows_of(bk, c)[:, :n2], rows_of(bk, c)[:, n2:], tc, ts) for c in range(nch)]
        brs = [t[:, :n2] for t in back]
        bis = [t[:, n2:] for t in back]
        y = _dot(f3_ref[...], jnp.concatenate([lanes(brs), lanes(bis)], axis=0))
        for b in range(2):
            for c in range(nch):
                conv = y[b * n1h:(b + 1) * n1h, c * n2:(c + 1) * n2] + z[b][c] * d_ref[n, c:c + 1, :]
                nxt = short_conv(n + 1, b, c) * conv
                if n == 0:
                    z[b][c] = nxt
                else:
                    gate = channel_rows(in_ref.at[3, b], c, n1h)
                    out_ref.at[b][pl.ds(c, n1h, stride=SUBLANES), :] = nxt * gate
    for b in range(2):
        o_ref[b, 0] = out_ref[b].astype(o_ref.dtype)


def _hyena(u_ct, kt_ct, short_w, short_b, hyena_d, n1):
    B, _, R, _ = u_ct.shape
    n1h, n2 = n1 // 2, LANES
    bf = lambda m: jnp.asarray(m, dtype=BF16)
    f1 = bf(_dft_left(n1, n1h, n1, -1))
    f1k = bf(_dft_left(n1, n1, n1, -1)[:, :n1])
    f2f, f2i = bf(_dft_right(n2, -1)), bf(_dft_right(n2, +1))
    f3 = bf(_dft_left(n1h, n1, n1, +1))
    ang = 2.0 * np.pi * ((np.arange(n1)[:, None] * np.arange(n2)[None, :]) % (n1 * n2)) / (n1 * n2)
    tc, ts = bf(np.cos(ang)), bf(np.sin(ang))
    wide = lambda t: jnp.broadcast_to(t[..., None], t.shape + (LANES,))
    sw = wide(short_w).reshape(3, 3, N_CGROUPS, SUBLANES, LANES)
    sb = wide(short_b).reshape(3, N_CGROUPS, SUBLANES, LANES)
    dd = wide(hyena_d).reshape(2, N_CGROUPS, SUBLANES, LANES)

    part = lambda j: pl.BlockSpec((2, 1, R, LANES), lambda g, p: (p, g + j * N_CGROUPS, 0, 0))
    const = lambda a: pl.BlockSpec(a.shape, lambda g, p: (0,) * a.ndim)
    mats = (f1, f1k, f2f, f2i, f3, tc, ts)
    return pl.pallas_call(
        functools.partial(_hyena_kernel, n1=n1),
        grid=(N_CGROUPS, B // 2),
        in_specs=[part(0), part(1), part(2), part(3),
                  pl.BlockSpec((2, 1, 2 * R, LANES), lambda g, p: (0, g, 0, 0)),
                  pl.BlockSpec((3, 3, None, SUBLANES, LANES), lambda g, p: (0, 0, g, 0, 0)),
                  pl.BlockSpec((3, None, SUBLANES, LANES), lambda g, p: (0, g, 0, 0)),
                  pl.BlockSpec((2, None, SUBLANES, LANES), lambda g, p: (0, g, 0, 0))]
                 + [const(m) for m in mats],
        out_specs=pl.BlockSpec((2, 1, R, LANES), lambda g, p: (p, g, 0, 0)),
        out_shape=jax.ShapeDtypeStruct((B, N_CGROUPS, R, LANES), BF16),
        scratch_shapes=[pltpu.VMEM((2, SUBLANES, n1, n2), BF16), pltpu.VMEM((2, SUBLANES, n1, n2), BF16),
                        pltpu.VMEM((4, 2, R, LANES), F32), pltpu.VMEM((2, R, LANES), F32)],
        compiler_params=_params("parallel", "arbitrary"),
        name="hyena",
    )(u_ct, u_ct, u_ct, u_ct, kt_ct, sw, sb, dd, *mats)


def _out_kernel(o1_ref, o4_ref, o16_ref, l1_ref, l4_ref, l16_ref, gates_ref, hg_ref, x_ref, mod_ref,
                fg_ref, e_ref, wpa_ref, wph_ref, wo_ref, out_ref, slab_ref, slab4_ref):
    tm = x_ref.shape[1]

    def token_order(ref, r):
        if r == 1:
            return ref[0].astype(F32)
        _, r1, r2 = DILATIONS
        nslab = ref.shape[-1] // LANES
        part = lambda rho, t: ref[rho, :, t * LANES:(t + 1) * LANES].astype(F32)
        if r == r1:
            planes = lambda rho, t: part(rho, t)
        else:
            step = r2 // r1
            for rho in range(r2):
                start = (rho % r1) * (tm // r1) + rho // r1
                for t in range(nslab):
                    slab4_ref.at[t][pl.ds(start, tm // r2, stride=step), :] = part(rho, t)
            planes = lambda rho, t: slab4_ref[t, rho * (tm // r1):(rho + 1) * (tm // r1), :]
        for rho in range(r1):
            for t in range(nslab):
                slab_ref.at[t][pl.ds(rho, tm // r1, stride=r1), :] = planes(rho, t)
        return jnp.concatenate([slab_ref[t] for t in range(nslab)], axis=1)

    stats = [token_order(ref, r) for ref, r in zip((l1_ref, l4_ref, l16_ref), DILATIONS)]
    mx = jnp.maximum(jnp.maximum(stats[0], stats[1]), stats[2])
    w = [jnp.exp2(st - mx) for st in stats]
    rowsum = [pltpu.roll(st, LANES - STAT_L_LANE, 1) for st in stats]
    tot = w[0] * rowsum[0] + w[1] * rowsum[1] + w[2] * rowsum[2]
    head_lane = lax.broadcasted_iota(jnp.int32, mx.shape, 1) < N_HEADS
    heads = lambda a: _dot(jnp.where(head_lane, a / tot, 0.0).astype(BF16), e_ref[...])
    oa = heads(w[0]) * token_order(o1_ref, 1)
    oa = oa + heads(w[1]) * token_order(o4_ref, 4)
    oa = oa + heads(w[2]) * token_order(o16_ref, 16)
    gate = lambda c0, width: gates_ref[0, :, c0:c0 + width].astype(F32)
    ya = _dot((oa * gate(G_GA, D_ATTN)).astype(BF16), wpa_ref[...])
    hg = hg_ref[0].astype(F32)
    yh = jnp.concatenate(
        [lax.dot_general(hg[:, j * SUBLANES:(j + 1) * SUBLANES, :].reshape(D_HYENA, LANES).astype(BF16),
                         wph_ref[...], TN_DIMS, preferred_element_type=F32) for j in range(tm // LANES)], axis=0)
    mixed = gate(G_MA, D_MODEL) * ya + gate(G_MH, D_MODEL) * yh
    y = x_ref[0] + mod_ref[0, 2:3, :] * _dot(mixed.astype(BF16), wo_ref[...])
    ms = jnp.mean(y * y, axis=-1, keepdims=True)
    out_ref[0] = y * lax.rsqrt(ms + EPS) * fg_ref[...]


def _out_proj(o_pat, lse_pat, gates, hg_ct, x, mod, final_g, wpa, wph, wo, tm):
    B, L, _ = x.shape
    row = lambda width: pl.BlockSpec((1, tm, width), lambda b, i: (b, i, 0))
    strided = lambda r, width: pl.BlockSpec((None, r, tm // r, width), lambda b, i: (b, 0, i, 0))
    const = lambda shape: pl.BlockSpec(shape, lambda b, i: (0,) * len(shape))
    expand = np.zeros((LANES, D_ATTN), np.float32)
    expand[np.arange(D_ATTN) // HEAD_DIM, np.arange(D_ATTN)] = 1.0
    return pl.pallas_call(
        _out_kernel,
        grid=(B, L // tm),
        in_specs=[strided(r, D_ATTN) for r in DILATIONS] + [strided(r, LANES) for r in DILATIONS]
                 + [row(D_GATES),
                    pl.BlockSpec((1, N_CGROUPS, tm // LANES * SUBLANES, LANES), lambda b, i: (b, 0, i, 0)),
                    row(D_MODEL),
                    pl.BlockSpec((1, 3, D_MODEL), lambda b, i: (b, 0, 0)),
                    const((1, D_MODEL)), const((LANES, D_ATTN)),
                    const((D_ATTN, D_MODEL)), const((D_HYENA, D_MODEL)), const((D_MODEL, D_MODEL))],
        out_specs=row(D_MODEL),
        out_shape=jax.ShapeDtypeStruct((B, L, D_MODEL), F32),
        scratch_shapes=[pltpu.VMEM((D_ATTN // LANES, tm, LANES), F32), pltpu.VMEM((D_ATTN // LANES, tm, LANES), F32)],
        compiler_params=_params("parallel", "parallel"),
        name="out_proj",
    )(*o_pat, *lse_pat, gates, hg_ct, x, mod, final_g.reshape(1, -1), jnp.asarray(expand, dtype=BF16),
      wpa, wph, wo)


def _trunk(x, mod, wts):
    B, L, _ = x.shape
    assert L % 512 == 0 and B % 2 == 0
    tm = 512
    *qkv, gates, u_ct = _inproj(x, mod, wts["norm_g"], wts["w_rows"], wts["w_hy_t"], tm)
    pats = [_band_attention(t, wts["rel_bias"]) for t in qkv]
    kt_ct = _filters(L, *wts["filt"], tr=512)
    hg_ct = _hyena(u_ct, kt_ct, wts["short_w"], wts["short_b"], wts["hyena_d"], 2 * L // LANES)
    return _out_proj([p[0] for p in pats], [p[1] for p in pats], gates, hg_ct, x, mod, wts["final_g"],
                     wts["w_proj_attn"], wts["w_proj_hyena"], wts["w_out"], tm)


def kernel(x_prompt, x_sample, c_prompt, c_sample, w_ada, b_ada, norm_g, w_in, short_w, short_b, filt_w1, filt_b1,
           filt_freq1, filt_w2, filt_b2, filt_freq2, filt_w3, hyena_d, w_proj_attn, w_proj_hyena, w_out, rel_bias,
           final_g):
    assert w_ada.shape[0] == 1, "single layer"
    bp, bs = c_prompt.shape[0], c_sample.shape[0]
    c_all = jnp.concatenate([c_prompt, c_sample], axis=0)
    rows = -(-(bp + bs) // SUBLANES) * SUBLANES
    c_all = jnp.pad(c_all, ((0, rows - bp - bs), (0, 0)))
    mod = _ada(c_all, w_ada[0], b_ada[0]).reshape(rows, 3, D_MODEL)
    w = w_in[0].astype(BF16)
    wts = dict(norm_g=norm_g[0],
               w_rows=jnp.concatenate([w[:, :C_U], w[:, C_MA:]], axis=1),
               w_hy_t=w[:, C_U:C_MA].T,
               short_w=short_w[0], short_b=short_b[0],
               filt=(filt_w1[0], filt_b1[0], filt_freq1[0], filt_w2[0], filt_b2[0], filt_freq2[0], filt_w3[0]),
               hyena_d=hyena_d[0], w_proj_attn=w_proj_attn[0].astype(BF16),
               w_proj_hyena=w_proj_hyena[0].astype(BF16), w_out=w_out[0].astype(BF16), rel_bias=rel_bias,
               final_g=final_g)
    y_prompt = _trunk(x_prompt, mod[:bp], wts)
    y_sample = _trunk(x_sample, mod[bp:bp + bs], wts)
    return (y_prompt, y_sample)
```
